```python
import jax, jax.numpy as jnp
from jax import lax
import numpy as np

D_MODEL = 1024
BATCH = 16
SEQ = 2048
DEPTH = 1

CHUNK = 64
HEAD_DIM = 64
SB_WIDTH = D_MODEL // 2
N_SB_HEADS = SB_WIDTH // HEAD_DIM
POOL_WINDOWS = (2, 4, 8, 16)
N_POOL_GROUPS = len(POOL_WINDOWS)
POOL_WIDTH = D_MODEL // 2
POOL_GROUP_DIM = POOL_WIDTH // N_POOL_GROUPS
N_BRANCHES = 2
MIX_PROJ_WIDTH = 3 * SB_WIDTH + POOL_WIDTH + N_BRANCHES * D_MODEL
Q_BLOCK = 128
N_EXPERT_GROUPS = 4
EXPERTS_PER_GROUP = 8
N_EXPERTS = N_EXPERT_GROUPS * EXPERTS_PER_GROUP
TOP_K_INNER = 2
D_EXPERT = D_MODEL // 2
MOE_BLOCK = 256
N_MOD = 6
EPS = 1e-6

kernel_name = 'hybrid_stickbreak_pool_hmoe_block'


def rms_norm(x, g):
    xf = x.astype(jnp.float32)
    y = xf * lax.rsqrt(jnp.mean(xf * xf, axis=-1, keepdims=True) + EPS)
    return (y * g.astype(jnp.float32)).astype(x.dtype)


def modulate(h, shift, scale):
    return h * (1.0 + scale[:, None, :]) + shift[:, None, :]


def stick_breaking_attention(q, k, v):
    seq = q.shape[2]
    inv_sqrt_d = HEAD_DIM ** -0.5
    outs = []
    for blk in range(seq // Q_BLOCK):
        q0 = blk * Q_BLOCK
        kv = q0 + Q_BLOCK
        z = jnp.einsum('bhtd,bhsd->bhts', q[:, :, q0:kv], k[:, :, :kv]).astype(jnp.float32) * inv_sqrt_d
        t_pos = q0 + jnp.arange(Q_BLOCK)[:, None]
        s_pos = jnp.arange(kv)[None, :]
        past = s_pos < t_pos
        log_stay = jnp.where(past, jax.nn.log_sigmoid(-z), 0.0)
        log_between = lax.cumsum(log_stay, axis=3, reverse=True) - log_stay
        a = jnp.where(past, jnp.exp(jax.nn.log_sigmoid(z) + log_between), 0.0)
        outs.append(jnp.einsum('bhts,bhsd->bhtd', a.astype(v.dtype), v[:, :, :kv]))
    return jnp.concatenate(outs, axis=2)


def multiscale_pool(p):
    b, s, _ = p.shape
    pg = p.reshape(b, s, N_POOL_GROUPS, POOL_GROUP_DIM).astype(jnp.float32)
    cs = jnp.concatenate([jnp.zeros((b, 1, N_POOL_GROUPS, POOL_GROUP_DIM), jnp.float32),
                          jnp.cumsum(pg, axis=1)], axis=1)
    t = jnp.arange(s)
    outs = []
    for g, w in enumerate(POOL_WINDOWS):
        lo = jnp.maximum(t + 1 - w, 0)
        win_sum = cs[:, 1:, g] - cs[:, lo, g]
        count = (t + 1 - lo).astype(jnp.float32)
        outs.append(win_sum / count[None, :, None])
    pooled = jnp.stack(outs, axis=2)
    return (pooled - pg).astype(p.dtype)


def hybrid_mixer(h, w_in, pool_w, pool_scale, w_branch_sb, w_branch_pool, w_out):
    b, s, _ = h.shape
    proj = h @ w_in
    q, k, v, p, gate_sb, gate_pool = jnp.split(
        proj, [SB_WIDTH, 2 * SB_WIDTH, 3 * SB_WIDTH, 3 * SB_WIDTH + POOL_WIDTH,
               3 * SB_WIDTH + POOL_WIDTH + D_MODEL], axis=-1)

    def heads(a):
        return a.reshape(b, s, N_SB_HEADS, HEAD_DIM).transpose(0, 2, 1, 3)

    o_sb = stick_breaking_attention(heads(q), heads(k), heads(v))
    o_sb = o_sb.transpose(0, 2, 1, 3).reshape(b, s, SB_WIDTH)

    pooled = multiscale_pool(p).reshape(b, s, N_POOL_GROUPS, POOL_GROUP_DIM)
    o_pool = jnp.einsum('bsgc,gcd->bsgd', pooled, pool_w).reshape(b, s, POOL_WIDTH) * pool_scale

    merged = (jax.nn.sigmoid(gate_sb) * (o_sb @ w_branch_sb)
              + jax.nn.sigmoid(gate_pool) * (o_pool @ w_branch_pool))
    return merged @ w_out


def hierarchical_moe(h, w_rg, b_rg, w_re, b_re, w_gate, w_up, w_down):
    b, s, d = h.shape
    n = b * s
    t = h.reshape(n, d)
    rows = jnp.arange(n)
    grp_logits = (t @ w_rg).astype(jnp.float32) + b_rg.astype(jnp.float32)
    grp_prob = jax.nn.softmax(grp_logits, axis=-1)
    _, gsel = lax.top_k(grp_logits, 1)
    p_group = grp_prob[rows, gsel[:, 0]]
    exp_logits = jnp.einsum('nd,gde->nge', t, w_re).astype(jnp.float32) + b_re.astype(jnp.float32)
    in_group = exp_logits[rows, gsel[:, 0]]
    top_val, top_idx = lax.top_k(in_group, TOP_K_INNER)
    combine = p_group[:, None] * jax.nn.softmax(top_val, axis=-1)
    expert_id = gsel * EXPERTS_PER_GROUP + top_idx

    nk = n * TOP_K_INNER
    flat_e = expert_id.reshape(nk)
    flat_tok = jnp.repeat(rows, TOP_K_INNER)
    flat_w = combine.reshape(nk)
    order = jnp.argsort(flat_e)
    sorted_e = flat_e[order]
    counts = jnp.zeros((N_EXPERTS,), jnp.int32).at[flat_e].add(1)
    padded = ((counts + MOE_BLOCK - 1) // MOE_BLOCK) * MOE_BLOCK
    pad_end = jnp.cumsum(padded)
    pad_start = pad_end - padded
    start = jnp.cumsum(counts) - counts
    dest = pad_start[sorted_e] + (jnp.arange(nk) - start[sorted_e])
    n_slots = ((nk + MOE_BLOCK - 1) // MOE_BLOCK + N_EXPERTS) * MOE_BLOCK
    n_blocks = n_slots // MOE_BLOCK
    slot_tok = jnp.zeros((n_slots,), jnp.int32).at[dest].set(flat_tok[order])
    slot_w = jnp.zeros((n_slots,), flat_w.dtype).at[dest].set(flat_w[order])
    block_e = jnp.minimum(jnp.searchsorted(pad_end, jnp.arange(n_blocks) * MOE_BLOCK, side='right'),
                          N_EXPERTS - 1)
    xs = t[slot_tok].reshape(n_blocks, MOE_BLOCK, d)

    def expert_block(args):
        xb, e = args
        return (jax.nn.silu(xb @ w_gate[e]) * (xb @ w_up[e])) @ w_down[e]

    ys = lax.map(expert_block, (xs, block_e)).reshape(n_slots, d)
    out = jnp.zeros((n, d), t.dtype).at[slot_tok].add(ys * slot_w[:, None].astype(t.dtype))
    return out.reshape(b, s, d)


def setup_inputs(seed: int = 0) -> dict:
    key = jax.random.key(seed)
    ks = jax.random.split(key, 20)
    f32 = jnp.float32

    def nrm(k, shape, scale):
        return jax.random.normal(k, shape, f32) * scale

    L = DEPTH
    return {
        'x': nrm(ks[0], (BATCH, SEQ, D_MODEL), 1.0),
        'c': nrm(ks[1], (BATCH, D_MODEL), 1.0),
        'norm1_g': 1.0 + nrm(ks[2], (L, D_MODEL), 0.05),
        'norm2_g': 1.0 + nrm(ks[3], (L, D_MODEL), 0.05),
        'w_ada': nrm(ks[4], (L, D_MODEL, N_MOD * D_MODEL), 0.5 * D_MODEL ** -0.5),
        'b_ada': nrm(ks[5], (L, N_MOD * D_MODEL), 0.01),
        'w_in': nrm(ks[6], (L, D_MODEL, MIX_PROJ_WIDTH), D_MODEL ** -0.5),
        'pool_w': nrm(ks[7], (L, N_POOL_GROUPS, POOL_GROUP_DIM, POOL_GROUP_DIM), POOL_GROUP_DIM ** -0.5),
        'pool_scale': 1.0 + nrm(ks[8], (L, POOL_WIDTH), 0.1),
        'w_branch_sb': nrm(ks[9], (L, SB_WIDTH, D_MODEL), SB_WIDTH ** -0.5),
        'w_branch_pool': nrm(ks[10], (L, POOL_WIDTH, D_MODEL), POOL_WIDTH ** -0.5),
        'w_out': nrm(ks[11], (L, D_MODEL, D_MODEL), D_MODEL ** -0.5),
        'w_route_group': nrm(ks[12], (L, D_MODEL, N_EXPERT_GROUPS), D_MODEL ** -0.5),
        'b_route_group': nrm(ks[13], (L, N_EXPERT_GROUPS), 0.01),
        'w_route_expert': nrm(ks[14], (L, N_EXPERT_GROUPS, D_MODEL, EXPERTS_PER_GROUP), D_MODEL ** -0.5),
        'b_route_expert': nrm(ks[15], (L, N_EXPERT_GROUPS, EXPERTS_PER_GROUP), 0.01),
        'w_exp_gate': nrm(ks[16], (L, N_EXPERTS, D_MODEL, D_EXPERT), D_MODEL ** -0.5),
        'w_exp_up': nrm(ks[17], (L, N_EXPERTS, D_MODEL, D_EXPERT), D_MODEL ** -0.5),
        'w_exp_down': nrm(ks[18], (L, N_EXPERTS, D_EXPERT, D_MODEL), D_EXPERT ** -0.5),
        'norm_f_g': 1.0 + nrm(ks[19], (D_MODEL,), 0.05),
    }


def reference(x, c, norm1_g, norm2_g, w_ada, b_ada, w_in, pool_w, pool_scale,
              w_branch_sb, w_branch_pool, w_out, w_route_group, b_route_group,
              w_route_expert, b_route_expert, w_exp_gate, w_exp_up, w_exp_down, norm_f_g):
    c_act = jax.nn.silu(c)
    for layer in range(DEPTH):
        mods = c_act @ w_ada[layer] + b_ada[layer]
        shift1, scale1, gate1, shift2, scale2, gate2 = jnp.split(mods, N_MOD, axis=-1)
        h = modulate(rms_norm(x, norm1_g[layer]), shift1, scale1)
        x = x + gate1[:, None, :] * hybrid_mixer(
            h, w_in[layer], pool_w[layer], pool_scale[layer],
            w_branch_sb[layer], w_branch_pool[layer], w_out[layer])
        h = modulate(rms_norm(x, norm2_g[layer]), shift2, scale2)
        x = x + gate2[:, None, :] * hierarchical_moe(
            h, w_route_group[layer], b_route_group[layer], w_route_expert[layer],
            b_route_expert[layer], w_exp_gate[layer], w_exp_up[layer], w_exp_down[layer])
    return rms_norm(x, norm_f_g)
```

```python
import functools

import jax
import jax.numpy as jnp
from jax import lax
from jax.experimental import pallas as pl
from jax.experimental.pallas import tpu as pltpu

F32 = jnp.float32
BF16 = jnp.bfloat16

D_MODEL = 1024
BATCH = 16
SEQ = 2048
N_TOK = BATCH * SEQ
HEAD_DIM = 64
SB_WIDTH = 512
POOL_WIDTH = 512
POOL_WINDOWS = (2, 4, 8, 16)
POOL_GROUP_DIM = 128
MIX_PROJ_WIDTH = 4096
N_EXPERT_GROUPS = 4
EXPERTS_PER_GROUP = 8
N_EXPERTS = 32
D_EXPERT = 512
EPS = 1e-6

LANES = 128
HALO = 16
TM_PROJ = 512
TQ = 256
TK = 256
TN_ROUTE = 512
TT_DISP = 1024
TB_EXP = 256
TC_COMB = 256
N_BLOCKS = (2 * N_TOK) // TB_EXP + N_EXPERTS
N_SLOTS = N_BLOCKS * TB_EXP
ROUTE_ROWS = 40
NEG_BIG = -1e30
VMEM_LIMIT = 56 * 1024 * 1024


def _cparams(n_axes):
    return pltpu.CompilerParams(dimension_semantics=("arbitrary",) * n_axes,
                                vmem_limit_bytes=VMEM_LIMIT)


def _ada_kernel(c_ref, w_ref, b_ref, o_ref):
    c = c_ref[...]
    ca = (c * jax.nn.sigmoid(c)).astype(BF16)
    o_ref[...] = jnp.dot(ca, w_ref[...].astype(BF16), preferred_element_type=F32) + b_ref[...]


def _ada(c, w_ada, b_ada):
    n_out = w_ada.shape[1]
    tn = 1024
    return pl.pallas_call(
        _ada_kernel,
        grid=(n_out // tn,),
        in_specs=[pl.BlockSpec((BATCH, D_MODEL), lambda j: (0, 0)),
                  pl.BlockSpec((D_MODEL, tn), lambda j: (0, j)),
                  pl.BlockSpec((1, tn), lambda j: (0, j))],
        out_specs=pl.BlockSpec((BATCH, tn), lambda j: (0, j)),
        out_shape=jax.ShapeDtypeStruct((BATCH, n_out), F32),
        compiler_params=_cparams(1),
        name="ada",
    )(c, w_ada, b_ada.reshape(1, n_out))


def _rms_mod(x, g, shift, scale):
    ms = jnp.mean(x * x, axis=-1, keepdims=True)
    y = x * lax.rsqrt(ms + EPS) * g
    return y * (1.0 + scale) + shift


def _inproj_kernel(x_ref, g_ref, sh_ref, sc_ref, w_ref, o_ref):
    h = _rms_mod(x_ref[...], g_ref[...], sh_ref[...], sc_ref[...]).astype(BF16)
    for j in range(MIX_PROJ_WIDTH // 1024):
        o_ref[:, j * 1024:(j + 1) * 1024] = jnp.dot(
            h, w_ref[:, j * 1024:(j + 1) * 1024], preferred_element_type=F32).astype(BF16)


def _inproj(x2, g1, mods4, w_in_bf):
    tiles_per_seq = SEQ // TM_PROJ
    mod_spec = lambda k: pl.BlockSpec((None, None, 1, D_MODEL),
                                      lambda i: (i // tiles_per_seq, k, 0, 0))
    return pl.pallas_call(
        _inproj_kernel,
        grid=(N_TOK // TM_PROJ,),
        in_specs=[pl.BlockSpec((TM_PROJ, D_MODEL), lambda i: (i, 0)),
                  pl.BlockSpec((1, D_MODEL), lambda i: (0, 0)),
                  mod_spec(0), mod_spec(1),
                  pl.BlockSpec((D_MODEL, MIX_PROJ_WIDTH), lambda i: (0, 0))],
        out_specs=pl.BlockSpec((TM_PROJ, MIX_PROJ_WIDTH), lambda i: (i, 0)),
        out_shape=jax.ShapeDtypeStruct((N_TOK, MIX_PROJ_WIDTH), BF16),
        compiler_params=_cparams(1),
        name="inproj",
    )(x2, g1, mods4, mods4, w_in_bf)


def _attn_kernel(q_ref, k_ref, v_ref, o_ref):
    qi = pl.program_id(2)
    q = q_ref[...]
    lane_q = lax.broadcasted_iota(jnp.int32, (TQ, LANES), 1)
    lane_k = lax.broadcasted_iota(jnp.int32, (TK, LANES), 1)
    row = lax.broadcasted_iota(jnp.int32, (TQ, TK), 0)
    col = lax.broadcasted_iota(jnp.int32, (TQ, TK), 1)
    past = col < row
    upper = jnp.where(row > col, 1.0, 0.0).astype(BF16)
    zero_q = jnp.zeros_like(q)
    q_heads = (jnp.where(lane_q < HEAD_DIM, q, zero_q), jnp.where(lane_q >= HEAD_DIM, q, zero_q))

    def block(kb, carry, diag):
        acc, r0, r1 = carry
        start = pl.multiple_of(kb * TK, TK)
        k = k_ref[pl.ds(start, TK), :]
        v = v_ref[pl.ds(start, TK), :]
        zero_v = jnp.zeros_like(v)
        v_heads = (jnp.where(lane_k < HEAD_DIM, v, zero_v), jnp.where(lane_k >= HEAD_DIM, v, zero_v))
        new_r = []
        for hh, r in enumerate((r0, r1)):
            z = lax.dot_general(q_heads[hh], k, (((1,), (1,)), ((), ())), preferred_element_type=F32)
            sp = jnp.maximum(z, 0.0) + jnp.log(1.0 + jnp.exp(-jnp.abs(z)))
            ls = -sp
            if diag:
                ls = jnp.where(past, ls, 0.0)
            lb = jnp.dot(ls.astype(BF16), upper, preferred_element_type=F32)
            a = jnp.exp((z - sp) + lb + r)
            if diag:
                a = jnp.where(past, a, 0.0)
            acc = acc + jnp.dot(a.astype(BF16), v_heads[hh], preferred_element_type=F32)
            new_r.append(r + jnp.sum(ls, axis=-1, keepdims=True))
        return acc, new_r[0], new_r[1]

    init = (jnp.zeros((TQ, LANES), F32), jnp.zeros((TQ, 1), F32), jnp.zeros((TQ, 1), F32))
    carry = block(qi, init, True)
    carry = lax.fori_loop(0, qi, lambda j, c: block(qi - 1 - j, c, False), carry)
    o_ref[...] = carry[0].astype(BF16)


def _attention(proj):
    n_q = SEQ // TQ
    n_hp = SB_WIDTH // LANES
    return pl.pallas_call(
        _attn_kernel,
        grid=(BATCH, n_hp, n_q),
        in_specs=[pl.BlockSpec((TQ, LANES), lambda b, h, i: (b * n_q + i, h)),
                  pl.BlockSpec((SEQ, LANES), lambda b, h, i: (b, n_hp + h)),
                  pl.BlockSpec((SEQ, LANES), lambda b, h, i: (b, 2 * n_hp + h))],
        out_specs=pl.BlockSpec((TQ, LANES), lambda b, h, i: (b * n_q + i, h)),
        out_shape=jax.ShapeDtypeStruct((N_TOK, SB_WIDTH), BF16),
        compiler_params=_cparams(3),
        name="attn",
    )(proj, proj, proj)


def _merge_kernel(x_ref, osb_ref, p_ref, halo_ref, gsb_ref, gpool_ref,
                  poolw_ref, pscale_ref, wsb_ref, wpool_ref, wout_ref,
                  gate1_ref, shift2_ref, scale2_ref, g2_ref, wr_ref, br_ref,
                  x1_ref, h2_ref, lt_ref, pext_ref):
    i = pl.program_id(0)
    tiles_per_seq = SEQ // TM_PROJ
    t0 = (i % tiles_per_seq) * TM_PROJ
    halo = halo_ref[...].astype(F32)
    pext_ref[0:HALO, :] = jnp.where(t0 == 0, jnp.zeros_like(halo), halo)
    pext_ref[HALO:, :] = p_ref[...].astype(F32)
    pos = t0 + lax.broadcasted_iota(jnp.int32, (TM_PROJ, 1), 0)
    o_pool = []
    for g, w in enumerate(POOL_WINDOWS):
        cols = slice(g * POOL_GROUP_DIM, (g + 1) * POOL_GROUP_DIM)
        cur = pext_ref[HALO:, cols]
        win = cur
        for d in range(1, w):
            win = win + pext_ref[HALO - d:HALO - d + TM_PROJ, cols]
        count = jnp.minimum(pos + 1, w).astype(F32)
        pooled = win / count - cur
        og = jnp.dot(pooled.astype(BF16), poolw_ref[g], preferred_element_type=F32)
        o_pool.append(og * pscale_ref[:, cols])
    o_pool = jnp.concatenate(o_pool, axis=-1).astype(BF16)
    br_sb = jnp.dot(osb_ref[...], wsb_ref[...], preferred_element_type=F32)
    br_pool = jnp.dot(o_pool, wpool_ref[...], preferred_element_type=F32)
    merged = (jax.nn.sigmoid(gsb_ref[...].astype(F32)) * br_sb
              + jax.nn.sigmoid(gpool_ref[...].astype(F32)) * br_pool)
    mix = jnp.dot(merged.astype(BF16), wout_ref[...], preferred_element_type=F32)
    x1 = x_ref[...] + gate1_ref[...] * mix
    x1_ref[...] = x1
    h2 = _rms_mod(x1, g2_ref[...], shift2_ref[...], scale2_ref[...])
    h2_ref[...] = h2
    lt_ref[...] = lax.dot_general(wr_ref[...], h2.astype(BF16), (((1,), (1,)), ((), ())),
                                  preferred_element_type=F32) + br_ref[...]


def _merge(x2, o_sb, proj, mods4, g2, poolw_bf, pool_scale, wsb_bf, wpool_bf, wout_bf, wr_bf, br):
    tiles_per_seq = SEQ // TM_PROJ
    halo_blocks = TM_PROJ // HALO
    mod_spec = lambda k: pl.BlockSpec((None, None, 1, D_MODEL),
                                      lambda i: (i // tiles_per_seq, k, 0, 0))
    full = lambda shape: pl.BlockSpec(shape, lambda i: (0,) * len(shape))
    return pl.pallas_call(
        _merge_kernel,
        grid=(N_TOK // TM_PROJ,),
        in_specs=[pl.BlockSpec((TM_PROJ, D_MODEL), lambda i: (i, 0)),
                  pl.BlockSpec((TM_PROJ, SB_WIDTH), lambda i: (i, 0)),
                  pl.BlockSpec((TM_PROJ, POOL_WIDTH), lambda i: (i, 3)),
                  pl.BlockSpec((HALO, POOL_WIDTH),
                               lambda i: (jnp.maximum(i * halo_blocks - 1, 0), 3)),
                  pl.BlockSpec((TM_PROJ, D_MODEL), lambda i: (i, 2)),
                  pl.BlockSpec((TM_PROJ, D_MODEL), lambda i: (i, 3)),
                  full((len(POOL_WINDOWS), POOL_GROUP_DIM, POOL_GROUP_DIM)),
                  full((1, POOL_WIDTH)),
                  full((SB_WIDTH, D_MODEL)),
                  full((POOL_WIDTH, D_MODEL)),
                  full((D_MODEL, D_MODEL)),
                  mod_spec(2), mod_spec(3), mod_spec(4),
                  full((1, D_MODEL)),
                  full((ROUTE_ROWS, D_MODEL)),
                  full((ROUTE_ROWS, 1))],
        out_specs=[pl.BlockSpec((TM_PROJ, D_MODEL), lambda i: (i, 0)),
                   pl.BlockSpec((TM_PROJ, D_MODEL), lambda i: (i, 0)),
                   pl.BlockSpec((ROUTE_ROWS, TM_PROJ), lambda i: (0, i))],
        out_shape=[jax.ShapeDtypeStruct((N_TOK, D_MODEL), F32),
                   jax.ShapeDtypeStruct((N_TOK, D_MODEL), F32),
                   jax.ShapeDtypeStruct((ROUTE_ROWS, N_TOK), F32)],
        scratch_shapes=[pltpu.VMEM((TM_PROJ + HALO, POOL_WIDTH), F32)],
        compiler_params=_cparams(1),
        name="merge",
    )(x2, o_sb, proj, proj, proj, proj, poolw_bf, pool_scale, wsb_bf, wpool_bf, wout_bf,
      mods4, mods4, mods4, g2, wr_bf, br)


def _first_index_of(mask, idx, big):
    return jnp.min(jnp.where(mask, idx, big), axis=0, keepdims=True)


def _route_kernel(lt_ref, info_ref, cnt_ref, base_ref):
    step = pl.program_id(0)

    @pl.when(step == 0)
    def _():
        base_ref[...] = jnp.zeros_like(base_ref)

    tn = TN_ROUTE
    gl = lt_ref[0:8, :]
    row8 = lax.broadcasted_iota(jnp.int32, (8, tn), 0).astype(F32)
    gmax = jnp.max(gl, axis=0, keepdims=True)
    gsel = _first_index_of(gl == gmax, row8, 8.0)
    p_group = 1.0 / jnp.sum(jnp.exp(gl - gmax), axis=0, keepdims=True)
    ig = lt_ref[8:16, :]
    for g in range(1, N_EXPERT_GROUPS):
        ig = jnp.where(gsel == float(g), lt_ref[8 + 8 * g:16 + 8 * g, :], ig)
    m1 = jnp.max(ig, axis=0, keepdims=True)
    i1 = _first_index_of(ig == m1, row8, 8.0)
    rest = jnp.where(row8 == i1, -jnp.inf, ig)
    m2 = jnp.max(rest, axis=0, keepdims=True)
    i2 = _first_index_of(rest == m2, row8, 8.0)
    e2 = jnp.exp(m2 - m1)
    w_first = p_group / (1.0 + e2)
    w_second = p_group * e2 / (1.0 + e2)
    e_first = gsel * EXPERTS_PER_GROUP + i1
    e_second = gsel * EXPERTS_PER_GROUP + i2
    erow = lax.broadcasted_iota(jnp.int32, (N_EXPERTS, tn), 0).astype(F32)
    hit_first = erow == e_first
    hit_second = erow == e_second
    onehot = jnp.where(hit_first | hit_second, 1.0, 0.0)
    r_i = lax.broadcasted_iota(jnp.int32, (tn, tn), 0)
    c_i = lax.broadcasted_iota(jnp.int32, (tn, tn), 1)
    tri = jnp.where(r_i < c_i, 1.0, 0.0).astype(BF16)
    rank = jnp.dot(onehot.astype(BF16), tri, preferred_element_type=F32) + base_ref[...]
    rank_first = jnp.sum(jnp.where(hit_first, rank, 0.0), axis=0, keepdims=True)
    rank_second = jnp.sum(jnp.where(hit_second, rank, 0.0), axis=0, keepdims=True)
    new_base = base_ref[...] + jnp.sum(onehot, axis=1, keepdims=True)
    base_ref[...] = new_base
    cnt_ref[...] = new_base
    info = jnp.zeros((8, tn), F32)
    for r, val in enumerate((e_first, e_second, rank_first, rank_second, w_first, w_second)):
        info = jnp.where(row8 == float(r), val, info)
    info_ref[...] = info


def _route(logits_t):
    return pl.pallas_call(
        _route_kernel,
        grid=(N_TOK // TN_ROUTE,),
        in_specs=[pl.BlockSpec((ROUTE_ROWS, TN_ROUTE), lambda i: (0, i))],
        out_specs=[pl.BlockSpec((8, TN_ROUTE), lambda i: (0, i)),
                   pl.BlockSpec((N_EXPERTS, 1), lambda i: (0, 0))],
        out_shape=[jax.ShapeDtypeStruct((8, N_TOK), F32),
                   jax.ShapeDtypeStruct((N_EXPERTS, 1), F32)],
        scratch_shapes=[pltpu.VMEM((N_EXPERTS, 1), F32)],
        compiler_params=_cparams(1),
        name="route",
    )(logits_t)


def _row_copy(src_hbm, src_row, dst_hbm, dst_row, sem):
    return pltpu.make_async_copy(src_hbm.at[pl.ds(src_row, 1), :], dst_hbm.at[pl.ds(dst_row, 1), :], sem)


def _dispatch_kernel(s0_ref, s1_ref, h2_hbm, xs_in_hbm, xs_hbm, sem):
    del xs_in_hbm
    t0 = pl.program_id(0) * TT_DISP

    def issue(j, c):
        _row_copy(h2_hbm, t0 + j, xs_hbm, s0_ref[j], sem).start()
        _row_copy(h2_hbm, t0 + j, xs_hbm, s1_ref[j], sem).start()
        return c

    lax.fori_loop(0, TT_DISP, issue, 0, unroll=8)

    def drain(j, c):
        _row_copy(h2_hbm, t0 + j, xs_hbm, s0_ref[j], sem).wait()
        _row_copy(h2_hbm, t0 + j, xs_hbm, s1_ref[j], sem).wait()
        return c

    lax.fori_loop(0, TT_DISP, drain, 0, unroll=8)


def _dispatch(slot0, slot1, h2):
    smem = lambda: pl.BlockSpec((TT_DISP,), lambda i: (i,), memory_space=pltpu.SMEM)
    xs_zero = jnp.zeros((N_SLOTS, D_MODEL), F32)
    return pl.pallas_call(
        _dispatch_kernel,
        grid=(N_TOK // TT_DISP,),
        in_specs=[smem(), smem(),
                  pl.BlockSpec(memory_space=pl.ANY),
                  pl.BlockSpec(memory_space=pl.ANY)],
        out_specs=pl.BlockSpec(memory_space=pl.ANY),
        out_shape=jax.ShapeDtypeStruct((N_SLOTS, D_MODEL), F32),
        scratch_shapes=[pltpu.SemaphoreType.DMA(())],
        input_output_aliases={3: 0},
        compiler_params=_cparams(1),
        name="dispatch",
    )(slot0, slot1, h2, xs_zero)


def _experts_kernel(be_ref, nused_ref, xs_ref, wg_ref, wu_ref, wd_ref, ys_ref):
    used = pl.program_id(0) < nused_ref[0]

    @pl.when(used)
    def _():
        x = xs_ref[...].astype(BF16)
        g = jnp.dot(x, wg_ref[...], preferred_element_type=F32)
        u = jnp.dot(x, wu_ref[...], preferred_element_type=F32)
        mid = (g * jax.nn.sigmoid(g) * u).astype(BF16)
        ys_ref[...] = jnp.dot(mid, wd_ref[...], preferred_element_type=F32)

    @pl.when(jnp.logical_not(used))
    def _():
        ys_ref[...] = jnp.zeros_like(ys_ref)


def _experts(block_e, n_used, xs, wg_bf, wu_bf, wd_bf):
    blk = lambda i, be, nu: (jnp.minimum(i, nu[0] - 1), 0)
    wsel = lambda i, be, nu: (be[i], 0, 0)
    return pl.pallas_call(
        _experts_kernel,
        grid_spec=pltpu.PrefetchScalarGridSpec(
            num_scalar_prefetch=2,
            grid=(N_BLOCKS,),
            in_specs=[pl.BlockSpec((TB_EXP, D_MODEL), blk),
                      pl.BlockSpec((None, D_MODEL, D_EXPERT), wsel),
                      pl.BlockSpec((None, D_MODEL, D_EXPERT), wsel),
                      pl.BlockSpec((None, D_EXPERT, D_MODEL), wsel)],
            out_specs=pl.BlockSpec((TB_EXP, D_MODEL), lambda i, be, nu: (i, 0))),
        out_shape=jax.ShapeDtypeStruct((N_SLOTS, D_MODEL), F32),
        compiler_params=_cparams(1),
        name="experts",
    )(block_e, n_used, xs, wg_bf, wu_bf, wd_bf)


def _combine_kernel(s0_ref, s1_ref, ys_hbm, x1_ref, wts_ref, gate2_ref, gf_ref, o_ref,
                    y0_ref, y1_ref, sem):
    def issue(j, c):
        pltpu.make_async_copy(ys_hbm.at[pl.ds(s0_ref[j], 1), :], y0_ref.at[pl.ds(j, 1), :], sem).start()
        pltpu.make_async_copy(ys_hbm.at[pl.ds(s1_ref[j], 1), :], y1_ref.at[pl.ds(j, 1), :], sem).start()
        return c

    lax.fori_loop(0, TC_COMB, issue, 0, unroll=8)

    def drain(j, c):
        pltpu.make_async_copy(ys_hbm.at[pl.ds(s0_ref[j], 1), :], y0_ref.at[pl.ds(j, 1), :], sem).wait()
        pltpu.make_async_copy(ys_hbm.at[pl.ds(s1_ref[j], 1), :], y1_ref.at[pl.ds(j, 1), :], sem).wait()
        return c

    lax.fori_loop(0, TC_COMB, drain, 0, unroll=8)
    wts = wts_ref[...]
    moe = wts[:, 4:5] * y0_ref[...] + wts[:, 5:6] * y1_ref[...]
    x2 = x1_ref[...] + gate2_ref[...] * moe
    ms = jnp.mean(x2 * x2, axis=-1, keepdims=True)
    o_ref[...] = x2 * lax.rsqrt(ms + EPS) * gf_ref[...]


def _combine(slot0, slot1, ys, x1, wts_rows, mods4, gf):
    tiles_per_seq = SEQ // TC_COMB
    smem = lambda: pl.BlockSpec((TC_COMB,), lambda i: (i,), memory_space=pltpu.SMEM)
    return pl.pallas_call(
        _combine_kernel,
        grid=(N_TOK // TC_COMB,),
        in_specs=[smem(), smem(),
                  pl.BlockSpec(memory_space=pl.ANY),
                  pl.BlockSpec((TC_COMB, D_MODEL), lambda i: (i, 0)),
                  pl.BlockSpec((TC_COMB, 8), lambda i: (i, 0)),
                  pl.BlockSpec((None, None, 1, D_MODEL), lambda i: (i // tiles_per_seq, 5, 0, 0)),
                  pl.BlockSpec((1, D_MODEL), lambda i: (0, 0))],
        out_specs=pl.BlockSpec((TC_COMB, D_MODEL), lambda i: (i, 0)),
        out_shape=jax.ShapeDtypeStruct((N_TOK, D_MODEL), F32),
        scratch_shapes=[pltpu.VMEM((TC_COMB, D_MODEL), F32),
                        pltpu.VMEM((TC_COMB, D_MODEL), F32),
                        pltpu.SemaphoreType.DMA(())],
        compiler_params=_cparams(1),
        name="combine",
    )(slot0, slot1, ys, x1, wts_rows, mods4, gf)


def _layer(x2, mods4, norm1_g, norm2_g, w_in, pool_w, pool_scale, w_branch_sb, w_branch_pool,
           w_out, w_rg, b_rg, w_re, b_re, w_gate, w_up, w_down, norm_f_g):
    q_scale = jnp.where(jnp.arange(MIX_PROJ_WIDTH) < SB_WIDTH, HEAD_DIM ** -0.5, 1.0).astype(F32)
    w_in_bf = (w_in * q_scale[None, :]).astype(BF16)
    proj = _inproj(x2, norm1_g.reshape(1, D_MODEL), mods4, w_in_bf)
    o_sb = _attention(proj)

    wr = jnp.concatenate([w_rg.T, jnp.zeros((4, D_MODEL), F32),
                          w_re.transpose(0, 2, 1).reshape(N_EXPERTS, D_MODEL)], axis=0)
    br = jnp.concatenate([b_rg, jnp.full((4,), NEG_BIG, F32), b_re.reshape(N_EXPERTS)]).reshape(ROUTE_ROWS, 1)
    x1, h2, logits_t = _merge(
        x2, o_sb, proj, mods4, norm2_g.reshape(1, D_MODEL), pool_w.astype(BF16),
        pool_scale.reshape(1, POOL_WIDTH), w_branch_sb.astype(BF16), w_branch_pool.astype(BF16),
        w_out.astype(BF16), wr.astype(BF16), br)

    info, counts = _route(logits_t)
    counts = counts.reshape(N_EXPERTS).astype(jnp.int32)
    padded = ((counts + TB_EXP - 1) // TB_EXP) * TB_EXP
    pad_end = jnp.cumsum(padded)
    pad_start = pad_end - padded
    e01 = info[0:2].astype(jnp.int32)
    slots = pad_start[e01] + info[2:4].astype(jnp.int32)
    slot0, slot1 = slots[0], slots[1]
    block_e = jnp.minimum(jnp.searchsorted(pad_end, jnp.arange(N_BLOCKS, dtype=jnp.int32) * TB_EXP, side='right'),
                          N_EXPERTS - 1).astype(jnp.int32)
    n_used = (pad_end[-1:] // TB_EXP).astype(jnp.int32)

    xs = _dispatch(slot0, slot1, h2)
    ys = _experts(block_e, n_used, xs, w_gate.astype(BF16), w_up.astype(BF16), w_down.astype(BF16))
    return _combine(slot0, slot1, ys, x1, info.T, mods4, norm_f_g.reshape(1, D_MODEL))


def kernel(x, c, norm1_g, norm2_g, w_ada, b_ada, w_in, pool_w, pool_scale, w_branch_sb, w_branch_pool, w_out, w_route_group, b_route_group, w_route_expert, b_route_expert, w_exp_gate, w_exp_up, w_exp_down, norm_f_g):
    assert x.shape == (BATCH, SEQ, D_MODEL) and w_ada.shape[0] == 1
    mods = _ada(c, w_ada[0], b_ada[0])
    mods4 = mods.reshape(BATCH, 6, 1, D_MODEL)
    out = _layer(x.reshape(N_TOK, D_MODEL), mods4, norm1_g[0], norm2_g[0], w_in[0], pool_w[0],
                 pool_scale[0], w_branch_sb[0], w_branch_pool[0], w_out[0], w_route_group[0],
                 b_route_group[0], w_route_expert[0], b_route_expert[0], w_exp_gate[0], w_exp_up[0],
                 w_exp_down[0], norm_f_g)
    return out.reshape(BATCH, SEQ, D_MODEL)
```

```python
import functools

import jax
import jax.numpy as jnp
from jax import lax
from jax.experimental import pallas as pl
from jax.experimental.pallas import tpu as pltpu

F32 = jnp.float32
BF16 = jnp.bfloat16

D_MODEL = 1024
BATCH = 16
SEQ = 2048
N_TOK = BATCH * SEQ
HEAD_DIM = 64
SB_WIDTH = 512
POOL_WIDTH = 512
POOL_WINDOWS = (2, 4, 8, 16)
POOL_GROUP_DIM = 128
MIX_PROJ_WIDTH = 4096
N_EXPERT_GROUPS = 4
EXPERTS_PER_GROUP = 8
N_EXPERTS = 32
D_EXPERT = 512
EPS = 1e-6

LANES = 128
HALO = 16
TM_PROJ = 512
TQ = 256
TK = 256
TN_ROUTE = 512
TT_DISP = 1024
TB_EXP = 256
TC_COMB = 256
N_BLOCKS = (2 * N_TOK) // TB_EXP + N_EXPERTS
N_SLOTS = N_BLOCKS * TB_EXP
ROUTE_ROWS = 40
NEG_BIG = -1e30
VMEM_LIMIT = 56 * 1024 * 1024


def _cparams(n_axes):
    return pltpu.CompilerParams(dimension_semantics=("arbitrary",) * n_axes,
                                vmem_limit_bytes=VMEM_LIMIT)


def _ada_kernel(c_ref, w_ref, b_ref, o_ref):
    c = c_ref[...]
    ca = (c * jax.nn.sigmoid(c)).astype(BF16)
    o_ref[...] = jnp.dot(ca, w_ref[...].astype(BF16), preferred_element_type=F32) + b_ref[...]


def _ada(c, w_ada, b_ada):
    n_out = w_ada.shape[1]
    tn = 1024
    return pl.pallas_call(
        _ada_kernel,
        grid=(n_out // tn,),
        in_specs=[pl.BlockSpec((BATCH, D_MODEL), lambda j: (0, 0)),
                  pl.BlockSpec((D_MODEL, tn), lambda j: (0, j)),
                  pl.BlockSpec((1, tn), lambda j: (0, j))],
        out_specs=pl.BlockSpec((BATCH, tn), lambda j: (0, j)),
        out_shape=jax.ShapeDtypeStruct((BATCH, n_out), F32),
        compiler_params=_cparams(1),
        name="ada",
    )(c, w_ada, b_ada.reshape(1, n_out))


def _rms_mod(x, g, shift, scale):
    ms = jnp.mean(x * x, axis=-1, keepdims=True)
    y = x * lax.rsqrt(ms + EPS) * g
    return y * (1.0 + scale) + shift


def _inproj_kernel(x_ref, g_ref, sh_ref, sc_ref, w_ref, o_ref):
    h = _rms_mod(x_ref[...], g_ref[...], sh_ref[...], sc_ref[...]).astype(BF16)
    for j in range(MIX_PROJ_WIDTH // 1024):
        o_ref[:, j * 1024:(j + 1) * 1024] = jnp.dot(
            h, w_ref[:, j * 1024:(j + 1) * 1024], preferred_element_type=F32).astype(BF16)


def _inproj(x2, g1, mods4, w_in_bf):
    tiles_per_seq = SEQ // TM_PROJ
    mod_spec = lambda k: pl.BlockSpec((None, None, 1, D_MODEL),
                                      lambda i: (i // tiles_per_seq, k, 0, 0))
    return pl.pallas_call(
        _inproj_kernel,
        grid=(N_TOK // TM_PROJ,),
        in_specs=[pl.BlockSpec((TM_PROJ, D_MODEL), lambda i: (i, 0)),
                  pl.BlockSpec((1, D_MODEL), lambda i: (0, 0)),
                  mod_spec(0), mod_spec(1),
                  pl.BlockSpec((D_MODEL, MIX_PROJ_WIDTH), lambda i: (0, 0))],
        out_specs=pl.BlockSpec((TM_PROJ, MIX_PROJ_WIDTH), lambda i: (i, 0)),
        out_shape=jax.ShapeDtypeStruct((N_TOK, MIX_PROJ_WIDTH), BF16),
        compiler_params=_cparams(1),
        name="inproj",
    )(x2, g1, mods4, mods4, w_in_bf)


def _attn_kernel(q_ref, k_ref, v_ref, o_ref):
    qi = pl.program_id(2)
    q = q_ref[...]
    lane_q = lax.broadcasted_iota(jnp.int32, (TQ, LANES), 1)
    lane_k = lax.broadcasted_iota(jnp.int32, (TK, LANES), 1)
    row = lax.broadcasted_iota(jnp.int32, (TQ, TK), 0)
    col = lax.broadcasted_iota(jnp.int32, (TQ, TK), 1)
    past = col < row
    upper = jnp.where(row > col, 1.0, 0.0).astype(BF16)
    zero_q = jnp.zeros_like(q)
    q_heads = (jnp.where(lane_q < HEAD_DIM, q, zero_q), jnp.where(lane_q >= HEAD_DIM, q, zero_q))

    def block(kb, carry, diag):
        acc, r0, r1 = carry
        start = pl.multiple_of(kb * TK, TK)
        k = k_ref[pl.ds(start, TK), :]
        v = v_ref[pl.ds(start, TK), :]
        zero_v = jnp.zeros_like(v)
        v_heads = (jnp.where(lane_k < HEAD_DIM, v, zero_v), jnp.where(lane_k >= HEAD_DIM, v, zero_v))
        new_r = []
        for hh, r in enumerate((r0, r1)):
            z = lax.dot_general(q_heads[hh], k, (((1,), (1,)), ((), ())), preferred_element_type=F32)
            sp = jnp.maximum(z, 0.0) + jnp.log(1.0 + jnp.exp(-jnp.abs(z)))
            ls = -sp
            if diag:
                ls = jnp.where(past, ls, 0.0)
            lb = jnp.dot(ls.astype(BF16), upper, preferred_element_type=F32)
            a = jnp.exp((z - sp) + lb + r)
            if diag:
                a = jnp.where(past, a, 0.0)
            acc = acc + jnp.dot(a.astype(BF16), v_heads[hh], preferred_element_type=F32)
            new_r.append(r + jnp.sum(ls, axis=-1, keepdims=True))
        return acc, new_r[0], new_r[1]

    init = (jnp.zeros((TQ, LANES), F32), jnp.zeros((TQ, 1), F32), jnp.zeros((TQ, 1), F32))
    carry = block(qi, init, True)
    carry = lax.fori_loop(0, qi, lambda j, c: block(qi - 1 - j, c, False), carry)
    o_ref[...] = carry[0].astype(BF16)


def _attention(proj):
    n_q = SEQ // TQ
    n_hp = SB_WIDTH // LANES
    return pl.pallas_call(
        _attn_kernel,
        grid=(BATCH, n_hp, n_q),
        in_specs=[pl.BlockSpec((TQ, LANES), lambda b, h, i: (b * n_q + i, h)),
                  pl.BlockSpec((SEQ, LANES), lambda b, h, i: (b, n_hp + h)),
                  pl.BlockSpec((SEQ, LANES), lambda b, h, i: (b, 2 * n_hp + h))],
        out_specs=pl.BlockSpec((TQ, LANES), lambda b, h, i: (b * n_q + i, h)),
        out_shape=jax.ShapeDtypeStruct((N_TOK, SB_WIDTH), BF16),
        compiler_params=_cparams(3),
        name="attn",
    )(proj, proj, proj)


def _merge_kernel(x_ref, osb_ref, p_ref, halo_ref, gsb_ref, gpool_ref,
                  poolw_ref, pscale_ref, wsb_ref, wpool_ref, wout_ref,
                  gate1_ref, shift2_ref, scale2_ref, g2_ref, wr_ref, br_ref,
                  x1_ref, h2_ref, lt_ref, pext_ref):
    i = pl.program_id(0)
    tiles_per_seq = SEQ // TM_PROJ
    t0 = (i % tiles_per_seq) * TM_PROJ
    halo = halo_ref[...].astype(F32)
    pext_ref[0:HALO, :] = jnp.where(t0 == 0, jnp.zeros_like(halo), halo)
    pext_ref[HALO:, :] = p_ref[...].astype(F32)
    pos = t0 + lax.broadcasted_iota(jnp.int32, (TM_PROJ, 1), 0)
    o_pool = []
    for g, w in enumerate(POOL_WINDOWS):
        cols = slice(g * POOL_GROUP_DIM, (g + 1) * POOL_GROUP_DIM)
        cur = pext_ref[HALO:, cols]
        win = cur
        for d in range(1, w):
            win = win + pext_ref[HALO - d:HALO - d + TM_PROJ, cols]
        count = jnp.minimum(pos + 1, w).astype(F32)
        pooled = win / count - cur
        og = jnp.dot(pooled.astype(BF16), poolw_ref[g], preferred_element_type=F32)
        o_pool.append(og * pscale_ref[:, cols])
    o_pool = jnp.concatenate(o_pool, axis=-1).astype(BF16)
    br_sb = jnp.dot(osb_ref[...], wsb_ref[...], preferred_element_type=F32)
    br_pool = jnp.dot(o_pool, wpool_ref[...], preferred_element_type=F32)
    merged = (jax.nn.sigmoid(gsb_ref[...].astype(F32)) * br_sb
              + jax.nn.sigmoid(gpool_ref[...].astype(F32)) * br_pool)
    mix = jnp.dot(merged.astype(BF16), wout_ref[...], preferred_element_type=F32)
    x1 = x_ref[...] + gate1_ref[...] * mix
    x1_ref[...] = x1
    h2 = _rms_mod(x1, g2_ref[...], shift2_ref[...], scale2_ref[...])
    h2_ref[...] = h2
    lt_ref[...] = lax.dot_general(wr_ref[...], h2.astype(BF16), (((1,), (1,)), ((), ())),
                                  preferred_element_type=F32) + br_ref[...]


def _merge(x2, o_sb, proj, mods4, g2, poolw_bf, pool_scale, wsb_bf, wpool_bf, wout_bf, wr_bf, br):
    tiles_per_seq = SEQ // TM_PROJ
    halo_blocks = TM_PROJ // HALO
    mod_spec = lambda k: pl.BlockSpec((None, None, 1, D_MODEL),
                                      lambda i: (i // tiles_per_seq, k, 0, 0))
    full = lambda shape: pl.BlockSpec(shape, lambda i: (0,) * len(shape))
    return pl.pallas_call(
        _merge_kernel,
        grid=(N_TOK // TM_PROJ,),
        in_specs=[pl.BlockSpec((TM_PROJ, D_MODEL), lambda i: (i, 0)),
                  pl.BlockSpec((TM_PROJ, SB_WIDTH), lambda i: (i, 0)),
                  pl.BlockSpec((TM_PROJ, POOL_WIDTH), lambda i: (i, 3)),
                  pl.BlockSpec((HALO, POOL_WIDTH),
                               lambda i: (jnp.maximum(i * halo_blocks - 1, 0), 3)),
                  pl.BlockSpec((TM_PROJ, D_MODEL), lambda i: (i, 2)),
                  pl.BlockSpec((TM_PROJ, D_MODEL), lambda i: (i, 3)),
                  full((len(POOL_WINDOWS), POOL_GROUP_DIM, POOL_GROUP_DIM)),
                  full((1, POOL_WIDTH)),
                  full((SB_WIDTH, D_MODEL)),
                  full((POOL_WIDTH, D_MODEL)),
                  full((D_MODEL, D_MODEL)),
                  mod_spec(2), mod_spec(3), mod_spec(4),
                  full((1, D_MODEL)),
                  full((ROUTE_ROWS, D_MODEL)),
                  full((ROUTE_ROWS, 1))],
        out_specs=[pl.BlockSpec((TM_PROJ, D_MODEL), lambda i: (i, 0)),
                   pl.BlockSpec((TM_PROJ, D_MODEL), lambda i: (i, 0)),
                   pl.BlockSpec((ROUTE_ROWS, TM_PROJ), lambda i: (0, i))],
        out_shape=[jax.ShapeDtypeStruct((N_TOK, D_MODEL), F32),
                   jax.ShapeDtypeStruct((N_TOK, D_MODEL), F32),
                   jax.ShapeDtypeStruct((ROUTE_ROWS, N_TOK), F32)],
        scratch_shapes=[pltpu.VMEM((TM_PROJ + HALO, POOL_WIDTH), F32)],
        compiler_params=_cparams(1),
        name="merge",
    )(x2, o_sb, proj, proj, proj, proj, poolw_bf, pool_scale, wsb_bf, wpool_bf, wout_bf,
      mods4, mods4, mods4, g2, wr_bf, br)


def _first_index_of(mask, idx, big):
    return jnp.min(jnp.where(mask, idx, big), axis=0, keepdims=True)


def _route_kernel(lt_ref, info_ref, slot_ref, cnt_ref, base_ref, info_scr):
    phase = pl.program_id(0)
    step = pl.program_id(1)
    tn = TN_ROUTE
    tile = pl.ds(pl.multiple_of(step * tn, tn), tn)
    row8 = lax.broadcasted_iota(jnp.int32, (8, tn), 0).astype(F32)
    erow = lax.broadcasted_iota(jnp.int32, (N_EXPERTS, tn), 0).astype(F32)

    @pl.when((phase == 0) & (step == 0))
    def _():
        base_ref[...] = jnp.zeros_like(base_ref)

    @pl.when(phase == 0)
    def _():
        gl = lt_ref[0:8, :]
        gmax = jnp.max(gl, axis=0, keepdims=True)
        gsel = _first_index_of(gl == gmax, row8, 8.0)
        p_group = 1.0 / jnp.sum(jnp.exp(gl - gmax), axis=0, keepdims=True)
        ig = lt_ref[8:16, :]
        for g in range(1, N_EXPERT_GROUPS):
            ig = jnp.where(gsel == float(g), lt_ref[8 + 8 * g:16 + 8 * g, :], ig)
        m1 = jnp.max(ig, axis=0, keepdims=True)
        i1 = _first_index_of(ig == m1, row8, 8.0)
        rest = jnp.where(row8 == i1, -jnp.inf, ig)
        m2 = jnp.max(rest, axis=0, keepdims=True)
        i2 = _first_index_of(rest == m2, row8, 8.0)
        e2 = jnp.exp(m2 - m1)
        w_first = p_group / (1.0 + e2)
        w_second = p_group * e2 / (1.0 + e2)
        e_first = gsel * EXPERTS_PER_GROUP + i1
        e_second = gsel * EXPERTS_PER_GROUP + i2
        hit_first = erow == e_first
        hit_second = erow == e_second
        onehot = jnp.where(hit_first | hit_second, 1.0, 0.0)
        r_i = lax.broadcasted_iota(jnp.int32, (tn, tn), 0)
        c_i = lax.broadcasted_iota(jnp.int32, (tn, tn), 1)
        tri = jnp.where(r_i < c_i, 1.0, 0.0).astype(BF16)
        rank = jnp.dot(onehot.astype(BF16), tri, preferred_element_type=F32) + base_ref[...]
        rank_first = jnp.sum(jnp.where(hit_first, rank, 0.0), axis=0, keepdims=True)
        rank_second = jnp.sum(jnp.where(hit_second, rank, 0.0), axis=0, keepdims=True)
        base_ref[...] = base_ref[...] + jnp.sum(onehot, axis=1, keepdims=True)
        info = jnp.zeros((8, tn), F32)
        for r, val in enumerate((e_first, e_second, rank_first, rank_second, w_first, w_second)):
            info = jnp.where(row8 == float(r), val, info)
        info_scr[:, tile] = info

    @pl.when(phase == 1)
    def _():
        counts = base_ref[...]
        padded = jnp.floor((counts + (TB_EXP - 1.0)) * (1.0 / TB_EXP)) * TB_EXP
        r_e = lax.broadcasted_iota(jnp.int32, (N_EXPERTS, N_EXPERTS), 0)
        c_e = lax.broadcasted_iota(jnp.int32, (N_EXPERTS, N_EXPERTS), 1)
        padded_row = jnp.sum(jnp.where(r_e == c_e, padded, 0.0), axis=0, keepdims=True)
        pad_start = jnp.sum(jnp.where(c_e < r_e, padded_row, 0.0), axis=1, keepdims=True)
        info = info_scr[:, tile]
        slot_first = jnp.sum(jnp.where(erow == info[0:1, :], pad_start, 0.0), axis=0, keepdims=True) + info[2:3, :]
        slot_second = jnp.sum(jnp.where(erow == info[1:2, :], pad_start, 0.0), axis=0, keepdims=True) + info[3:4, :]
        slots = jnp.where(row8 == 0.0, slot_first, jnp.where(row8 == 1.0, slot_second, 0.0))
        slot_ref[...] = slots.astype(jnp.int32)
        info_ref[...] = info
        cnt_ref[...] = counts


def _route(logits_t):
    n_steps = N_TOK // TN_ROUTE
    return pl.pallas_call(
        _route_kernel,
        grid=(2, n_steps),
        in_specs=[pl.BlockSpec((ROUTE_ROWS, TN_ROUTE), lambda p, i: (0, jnp.where(p == 0, i, n_steps - 1)))],
        out_specs=[pl.BlockSpec((8, TN_ROUTE), lambda p, i: (0, i * p)),
                   pl.BlockSpec((8, TN_ROUTE), lambda p, i: (0, i * p)),
                   pl.BlockSpec((N_EXPERTS, 1), lambda p, i: (0, 0))],
        out_shape=[jax.ShapeDtypeStruct((8, N_TOK), F32),
                   jax.ShapeDtypeStruct((8, N_TOK), jnp.int32),
                   jax.ShapeDtypeStruct((N_EXPERTS, 1), F32)],
        scratch_shapes=[pltpu.VMEM((N_EXPERTS, 1), F32),
                        pltpu.VMEM((8, N_TOK), F32)],
        compiler_params=_cparams(2),
        name="route",
    )(logits_t)


def _dispatch_kernel(s0_ref, s1_ref, h2_ref, xs_in_hbm, xs_hbm, sem):
    del xs_in_hbm

    def row_copy(j, slot):
        return pltpu.make_async_copy(h2_ref.at[pl.ds(j, 1), :], xs_hbm.at[pl.ds(slot, 1), :], sem)

    def issue(j, c):
        row_copy(j, s0_ref[j]).start()
        row_copy(j, s1_ref[j]).start()
        return c

    lax.fori_loop(0, TT_DISP, issue, 0, unroll=8)

    def drain(j, c):
        row_copy(j, s0_ref[j]).wait()
        row_copy(j, s1_ref[j]).wait()
        return c

    lax.fori_loop(0, TT_DISP, drain, 0, unroll=8)


def _dispatch(slot0, slot1, h2):
    smem = lambda: pl.BlockSpec((TT_DISP,), lambda i: (i,), memory_space=pltpu.SMEM)
    xs_zero = jnp.zeros((N_SLOTS, D_MODEL), F32)
    return pl.pallas_call(
        _dispatch_kernel,
        grid=(N_TOK // TT_DISP,),
        in_specs=[smem(), smem(),
                  pl.BlockSpec((TT_DISP, D_MODEL), lambda i: (i, 0)),
                  pl.BlockSpec(memory_space=pl.ANY)],
        out_specs=pl.BlockSpec(memory_space=pl.ANY),
        out_shape=jax.ShapeDtypeStruct((N_SLOTS, D_MODEL), F32),
        scratch_shapes=[pltpu.SemaphoreType.DMA(())],
        input_output_aliases={3: 0},
        compiler_params=_cparams(1),
        name="dispatch",
    )(slot0, slot1, h2, xs_zero)


def _experts_kernel(be_ref, nused_ref, xs_ref, wg_ref, wu_ref, wd_ref, ys_ref):
    used = pl.program_id(0) < nused_ref[0]

    @pl.when(used)
    def _():
        x = xs_ref[...].astype(BF16)
        g = jnp.dot(x, wg_ref[...], preferred_element_type=F32)
        u = jnp.dot(x, wu_ref[...], preferred_element_type=F32)
        mid = (g * jax.nn.sigmoid(g) * u).astype(BF16)
        ys_ref[...] = jnp.dot(mid, wd_ref[...], preferred_element_type=F32)

    @pl.when(jnp.logical_not(used))
    def _():
        ys_ref[...] = jnp.zeros_like(ys_ref)


def _experts(block_e, n_used, xs, wg_bf, wu_bf, wd_bf):
    blk = lambda i, be, nu: (jnp.minimum(i, nu[0] - 1), 0)
    wsel = lambda i, be, nu: (be[i], 0, 0)
    return pl.pallas_call(
        _experts_kernel,
        grid_spec=pltpu.PrefetchScalarGridSpec(
            num_scalar_prefetch=2,
            grid=(N_BLOCKS,),
            in_specs=[pl.BlockSpec((TB_EXP, D_MODEL), blk),
                      pl.BlockSpec((None, D_MODEL, D_EXPERT), wsel),
                      pl.BlockSpec((None, D_MODEL, D_EXPERT), wsel),
                      pl.BlockSpec((None, D_EXPERT, D_MODEL), wsel)],
            out_specs=pl.BlockSpec((TB_EXP, D_MODEL), lambda i, be, nu: (i, 0))),
        out_shape=jax.ShapeDtypeStruct((N_SLOTS, D_MODEL), F32),
        compiler_params=_cparams(1),
        name="experts",
    )(block_e, n_used, xs, wg_bf, wu_bf, wd_bf)


def _combine_kernel(s0_ref, s1_ref, ys_hbm, x1_ref, wts_ref, gate2_ref, gf_ref, o_ref,
                    y0_ref, y1_ref, sem):
    def issue(j, c):
        pltpu.make_async_copy(ys_hbm.at[pl.ds(s0_ref[j], 1), :], y0_ref.at[pl.ds(j, 1), :], sem).start()
        pltpu.make_async_copy(ys_hbm.at[pl.ds(s1_ref[j], 1), :], y1_ref.at[pl.ds(j, 1), :], sem).start()
        return c

    lax.fori_loop(0, TC_COMB, issue, 0, unroll=8)

    def drain(j, c):
        pltpu.make_async_copy(ys_hbm.at[pl.ds(s0_ref[j], 1), :], y0_ref.at[pl.ds(j, 1), :], sem).wait()
        pltpu.make_async_copy(ys_hbm.at[pl.ds(s1_ref[j], 1), :], y1_ref.at[pl.ds(j, 1), :], sem).wait()
        return c

    lax.fori_loop(0, TC_COMB, drain, 0, unroll=8)
    wts = wts_ref[...]
    moe = wts[:, 4:5] * y0_ref[...] + wts[:, 5:6] * y1_ref[...]
    x2 = x1_ref[...] + gate2_ref[...] * moe
    ms = jnp.mean(x2 * x2, axis=-1, keepdims=True)
    o_ref[...] = x2 * lax.rsqrt(ms + EPS) * gf_ref[...]


def _combine(slot0, slot1, ys, x1, wts_rows, mods4, gf):
    tiles_per_seq = SEQ // TC_COMB
    smem = lambda: pl.BlockSpec((TC_COMB,), lambda i: (i,), memory_space=pltpu.SMEM)
    return pl.pallas_call(
        _combine_kernel,
        grid=(N_TOK // TC_COMB,),
        in_specs=[smem(), smem(),
                  pl.BlockSpec(memory_space=pl.ANY),
                  pl.BlockSpec((TC_COMB, D_MODEL), lambda i: (i, 0)),
                  pl.BlockSpec((TC_COMB, 8), lambda i: (i, 0)),
                  pl.BlockSpec((None, None, 1, D_MODEL), lambda i: (i // tiles_per_seq, 5, 0, 0)),
                  pl.BlockSpec((1, D_MODEL), lambda i: (0, 0))],
        out_specs=pl.BlockSpec((TC_COMB, D_MODEL), lambda i: (i, 0)),
        out_shape=jax.ShapeDtypeStruct((N_TOK, D_MODEL), F32),
        scratch_shapes=[pltpu.VMEM((TC_COMB, D_MODEL), F32),
                        pltpu.VMEM((TC_COMB, D_MODEL), F32),
                        pltpu.SemaphoreType.DMA(())],
        compiler_params=_cparams(1),
        name="combine",
    )(slot0, slot1, ys, x1, wts_rows, mods4, gf)


def _layer(x2, mods4, norm1_g, norm2_g, w_in, pool_w, pool_scale, w_branch_sb, w_branch_pool,
           w_out, w_rg, b_rg, w_re, b_re, w_gate, w_up, w_down, norm_f_g):
    q_scale = jnp.where(jnp.arange(MIX_PROJ_WIDTH) < SB_WIDTH, HEAD_DIM ** -0.5, 1.0).astype(F32)
    w_in_bf = (w_in * q_scale[None, :]).astype(BF16)
    proj = _inproj(x2, norm1_g.reshape(1, D_MODEL), mods4, w_in_bf)
    o_sb = _attention(proj)

    wr = jnp.concatenate([w_rg.T, jnp.zeros((4, D_MODEL), F32),
                          w_re.transpose(0, 2, 1).reshape(N_EXPERTS, D_MODEL)], axis=0)
    br = jnp.concatenate([b_rg, jnp.full((4,), NEG_BIG, F32), b_re.reshape(N_EXPERTS)]).reshape(ROUTE_ROWS, 1)
    x1, h2, logits_t = _merge(
        x2, o_sb, proj, mods4, norm2_g.reshape(1, D_MODEL), pool_w.astype(BF16),
        pool_scale.reshape(1, POOL_WIDTH), w_branch_sb.astype(BF16), w_branch_pool.astype(BF16),
        w_out.astype(BF16), wr.astype(BF16), br)

    info, slots, counts = _route(logits_t)
    slot0, slot1 = slots[0], slots[1]
    counts = counts.reshape(N_EXPERTS).astype(jnp.int32)
    pad_end = jnp.cumsum(((counts + TB_EXP - 1) // TB_EXP) * TB_EXP)
    block_first = jnp.arange(N_BLOCKS, dtype=jnp.int32) * TB_EXP
    block_e = jnp.minimum(jnp.sum((pad_end[None, :] <= block_first[:, None]).astype(jnp.int32), axis=1),
                          N_EXPERTS - 1).astype(jnp.int32)
    n_used = (pad_end[-1:] // TB_EXP).astype(jnp.int32)

    xs = _dispatch(slot0, slot1, h2)
    ys = _experts(block_e, n_used, xs, w_gate.astype(BF16), w_up.astype(BF16), w_down.astype(BF16))
    return _combine(slot0, slot1, ys, x1, info.T, mods4, norm_f_g.reshape(1, D_MODEL))


def kernel(x, c, norm1_g, norm2_g, w_ada, b_ada, w_in, pool_w, pool_scale, w_branch_sb, w_branch_pool, w_out, w_route_group, b_route_group, w_route_expert, b_route_expert, w_exp_gate, w_exp_up, w_exp_down, norm_f_g):
    assert x.shape == (BATCH, SEQ, D_MODEL) and w_ada.shape[0] == 1
    mods = _ada(c, w_ada[0], b_ada[0])
    mods4 = mods.reshape(BATCH, 6, 1, D_MODEL)
    out = _layer(x.reshape(N_TOK, D_MODEL), mods4, norm1_g[0], norm2_g[0], w_in[0], pool_w[0],
                 pool_scale[0], w_branch_sb[0], w_branch_pool[0], w_out[0], w_route_group[0],
                 b_route_group[0], w_route_expert[0], b_route_expert[0], w_exp_gate[0], w_exp_up[0],
                 w_exp_down[0], norm_f_g)
    return out.reshape(BATCH, SEQ, D_MODEL)
```

```python
import functools

import jax
import jax.numpy as jnp
from jax import lax
from jax.experimental import pallas as pl
from jax.experimental.pallas import tpu as pltpu

F32 = jnp.float32
BF16 = jnp.bfloat16

D_MODEL = 1024
BATCH = 16
SEQ = 2048
N_TOK = BATCH * SEQ
HEAD_DIM = 64
SB_WIDTH = 512
POOL_WIDTH = 512
POOL_WINDOWS = (2, 4, 8, 16)
POOL_GROUP_DIM = 128
MIX_PROJ_WIDTH = 4096
N_EXPERT_GROUPS = 4
EXPERTS_PER_GROUP = 8
N_EXPERTS = 32
D_EXPERT = 512
EPS = 1e-6

LANES = 128
HALO = 16
TM_PROJ = 512
TQ = 256
TK = 256
MXU_DIM = 256
ATTN_HEADS = 8
ATTN_WIDTH = ATTN_HEADS * HEAD_DIM
LOG2_E = 1.4426950408889634
TN_ROUTE = 512
TT_DISP = 1024
TB_EXP = 256
TC_COMB = 256
N_BLOCKS = (2 * N_TOK) // TB_EXP + N_EXPERTS
N_SLOTS = N_BLOCKS * TB_EXP
ROUTE_ROWS = 40
NEG_BIG = -1e30
VMEM_LIMIT = 56 * 1024 * 1024


def _cparams(n_axes):
    return pltpu.CompilerParams(dimension_semantics=("arbitrary",) * n_axes,
                                vmem_limit_bytes=VMEM_LIMIT)


def _ada_kernel(c_ref, w_ref, b_ref, o_ref):
    c = c_ref[...]
    ca = (c * jax.nn.sigmoid(c)).astype(BF16)
    o_ref[...] = jnp.dot(ca, w_ref[...].astype(BF16), preferred_element_type=F32) + b_ref[...]


def _ada(c, w_ada, b_ada):
    n_out = w_ada.shape[1]
    tn = 1024
    return pl.pallas_call(
        _ada_kernel,
        grid=(n_out // tn,),
        in_specs=[pl.BlockSpec((BATCH, D_MODEL), lambda j: (0, 0)),
                  pl.BlockSpec((D_MODEL, tn), lambda j: (0, j)),
                  pl.BlockSpec((1, tn), lambda j: (0, j))],
        out_specs=pl.BlockSpec((BATCH, tn), lambda j: (0, j)),
        out_shape=jax.ShapeDtypeStruct((BATCH, n_out), F32),
        compiler_params=_cparams(1),
        name="ada",
    )(c, w_ada, b_ada.reshape(1, n_out))


def _rms_mod(x, g, shift, scale):
    ms = jnp.mean(x * x, axis=-1, keepdims=True)
    y = x * lax.rsqrt(ms + EPS) * g
    return y * (1.0 + scale) + shift


def _inproj_kernel(x_ref, g_ref, sh_ref, sc_ref, w_ref, o_ref):
    h = _rms_mod(x_ref[...], g_ref[...], sh_ref[...], sc_ref[...]).astype(BF16)
    o_ref[:, 0:SB_WIDTH] = (jnp.dot(h, w_ref[:, 0:SB_WIDTH], preferred_element_type=F32) * LOG2_E).astype(BF16)
    for lo in range(SB_WIDTH, MIX_PROJ_WIDTH, SB_WIDTH):
        o_ref[:, lo:lo + SB_WIDTH] = jnp.dot(
            h, w_ref[:, lo:lo + SB_WIDTH], preferred_element_type=F32).astype(BF16)


def _inproj(x2, g1, mods4, w_in_bf):
    tiles_per_seq = SEQ // TM_PROJ
    mod_spec = lambda k: pl.BlockSpec((None, None, 1, D_MODEL),
                                      lambda i: (i // tiles_per_seq, k, 0, 0))
    return pl.pallas_call(
        _inproj_kernel,
        grid=(N_TOK // TM_PROJ,),
        in_specs=[pl.BlockSpec((TM_PROJ, D_MODEL), lambda i: (i, 0)),
                  pl.BlockSpec((1, D_MODEL), lambda i: (0, 0)),
                  mod_spec(0), mod_spec(1),
                  pl.BlockSpec((D_MODEL, MIX_PROJ_WIDTH), lambda i: (0, 0))],
        out_specs=pl.BlockSpec((TM_PROJ, MIX_PROJ_WIDTH), lambda i: (i, 0)),
        out_shape=jax.ShapeDtypeStruct((N_TOK, MIX_PROJ_WIDTH), BF16),
        compiler_params=_cparams(1),
        name="inproj",
    )(x2, g1, mods4, mods4, w_in_bf)


def _attn_kernel(q_ref, k_ref, v_ref, o_ref):
    qi = pl.program_id(2)
    head_q = lax.broadcasted_iota(jnp.int32, (TQ, MXU_DIM), 1) // HEAD_DIM
    head_k = lax.broadcasted_iota(jnp.int32, (TK, MXU_DIM), 1) // HEAD_DIM
    row = lax.broadcasted_iota(jnp.int32, (TQ, TK), 0)
    col = lax.broadcasted_iota(jnp.int32, (TQ, TK), 1)
    past = col < row
    upper = jnp.where(row > col, 1.0, 0.0).astype(BF16)
    heads_per_group = MXU_DIM // HEAD_DIM
    n_groups = ATTN_WIDTH // MXU_DIM
    q_heads = []
    for g in range(n_groups):
        qg = q_ref[:, g * MXU_DIM:(g + 1) * MXU_DIM]
        q_heads += [jnp.where(head_q == hh, qg, jnp.zeros_like(qg)) for hh in range(heads_per_group)]

    def block(kb, carry, diag):
        accs, rs = carry[:n_groups], carry[n_groups:]
        start = pl.multiple_of(kb * TK, TK)
        new_acc, new_r = [], []
        for g in range(n_groups):
            k = k_ref[pl.ds(start, TK), g * MXU_DIM:(g + 1) * MXU_DIM]
            v = v_ref[pl.ds(start, TK), g * MXU_DIM:(g + 1) * MXU_DIM]
            acc = accs[g]
            for hh in range(heads_per_group):
                r = rs[g * heads_per_group + hh]
                z2 = lax.dot_general(q_heads[g * heads_per_group + hh], k, (((1,), (1,)), ((), ())),
                                     preferred_element_type=F32)
                sp2 = jnp.maximum(z2, 0.0) + jnp.log(1.0 + jnp.exp2(-jnp.abs(z2))) * LOG2_E
                stay = jnp.where(past, sp2, 0.0) if diag else sp2
                cum = jnp.dot(stay.astype(BF16), upper, preferred_element_type=F32)
                a = jnp.exp2(z2 - (sp2 + cum + r))
                if diag:
                    a = jnp.where(past, a, 0.0)
                v_h = jnp.where(head_k == hh, v, jnp.zeros_like(v))
                acc = acc + jnp.dot(a.astype(BF16), v_h, preferred_element_type=F32)
                new_r.append(r + jnp.sum(stay, axis=-1, keepdims=True))
            new_acc.append(acc)
        return (*new_acc, *new_r)

    init = (jnp.zeros((TQ, MXU_DIM), F32),) * n_groups + (jnp.zeros((TQ, 1), F32),) * ATTN_HEADS
    carry = block(qi, init, True)
    carry = lax.fori_loop(0, qi, lambda j, c: block(qi - 1 - j, c, False), carry)
    for g in range(n_groups):
        o_ref[:, g * MXU_DIM:(g + 1) * MXU_DIM] = carry[g].astype(BF16)


def _attention(proj):
    n_q = SEQ // TQ
    n_hg = SB_WIDTH // ATTN_WIDTH
    return pl.pallas_call(
        _attn_kernel,
        grid=(BATCH, n_hg, n_q),
        in_specs=[pl.BlockSpec((TQ, ATTN_WIDTH), lambda b, h, i: (b * n_q + i, h)),
                  pl.BlockSpec((SEQ, ATTN_WIDTH), lambda b, h, i: (b, n_hg + h)),
                  pl.BlockSpec((SEQ, ATTN_WIDTH), lambda b, h, i: (b, 2 * n_hg + h))],
        out_specs=pl.BlockSpec((TQ, ATTN_WIDTH), lambda b, h, i: (b * n_q + i, h)),
        out_shape=jax.ShapeDtypeStruct((N_TOK, SB_WIDTH), BF16),
        compiler_params=_cparams(3),
        name="attn",
    )(proj, proj, proj)


def _merge_kernel(x_ref, osb_ref, p_ref, halo_ref, gsb_ref, gpool_ref,
                  poolw_ref, pscale_ref, wsb_ref, wpool_ref, wout_ref,
                  gate1_ref, shift2_ref, scale2_ref, g2_ref, wr_ref, br_ref,
                  x1_ref, h2_ref, lt_ref, pext_ref):
    i = pl.program_id(0)
    tiles_per_seq = SEQ // TM_PROJ
    t0 = (i % tiles_per_seq) * TM_PROJ
    halo = halo_ref[...].astype(F32)
    pext_ref[0:HALO, :] = jnp.where(t0 == 0, jnp.zeros_like(halo), halo)
    pext_ref[HALO:, :] = p_ref[...].astype(F32)
    pos = t0 + lax.broadcasted_iota(jnp.int32, (TM_PROJ, 1), 0)
    o_pool = []
    for g, w in enumerate(POOL_WINDOWS):
        cols = slice(g * POOL_GROUP_DIM, (g + 1) * POOL_GROUP_DIM)
        cur = pext_ref[HALO:, cols]
        win = cur
        for d in range(1, w):
            win = win + pext_ref[HALO - d:HALO - d + TM_PROJ, cols]
        count = jnp.minimum(pos + 1, w).astype(F32)
        pooled = win / count - cur
        og = jnp.dot(pooled.astype(BF16), poolw_ref[g], preferred_element_type=F32)
        o_pool.append(og * pscale_ref[:, cols])
    o_pool = jnp.concatenate(o_pool, axis=-1).astype(BF16)
    br_sb = jnp.dot(osb_ref[...], wsb_ref[...], preferred_element_type=F32)
    br_pool = jnp.dot(o_pool, wpool_ref[...], preferred_element_type=F32)
    merged = (jax.nn.sigmoid(gsb_ref[...].astype(F32)) * br_sb
              + jax.nn.sigmoid(gpool_ref[...].astype(F32)) * br_pool)
    mix = jnp.dot(merged.astype(BF16), wout_ref[...], preferred_element_type=F32)
    x1 = x_ref[...] + gate1_ref[...] * mix
    x1_ref[...] = x1
    h2 = _rms_mod(x1, g2_ref[...], shift2_ref[...], scale2_ref[...])
    h2_ref[...] = h2
    lt_ref[...] = lax.dot_general(wr_ref[...], h2.astype(BF16), (((1,), (1,)), ((), ())),
                                  preferred_element_type=F32) + br_ref[...]


def _merge(x2, o_sb, proj, mods4, g2, poolw_bf, pool_scale, wsb_bf, wpool_bf, wout_bf, wr_bf, br):
    tiles_per_seq = SEQ // TM_PROJ
    halo_blocks = TM_PROJ // HALO
    mod_spec = lambda k: pl.BlockSpec((None, None, 1, D_MODEL),
                                      lambda i: (i // tiles_per_seq, k, 0, 0))
    full = lambda shape: pl.BlockSpec(shape, lambda i: (0,) * len(shape))
    return pl.pallas_call(
        _merge_kernel,
        grid=(N_TOK // TM_PROJ,),
        in_specs=[pl.BlockSpec((TM_PROJ, D_MODEL), lambda i: (i, 0)),
                  pl.BlockSpec((TM_PROJ, SB_WIDTH), lambda i: (i, 0)),
                  pl.BlockSpec((TM_PROJ, POOL_WIDTH), lambda i: (i, 3)),
                  pl.BlockSpec((HALO, POOL_WIDTH),
                               lambda i: (jnp.maximum(i * halo_blocks - 1, 0), 3)),
                  pl.BlockSpec((TM_PROJ, D_MODEL), lambda i: (i, 2)),
                  pl.BlockSpec((TM_PROJ, D_MODEL), lambda i: (i, 3)),
                  full((len(POOL_WINDOWS), POOL_GROUP_DIM, POOL_GROUP_DIM)),
                  full((1, POOL_WIDTH)),
                  full((SB_WIDTH, D_MODEL)),
                  full((POOL_WIDTH, D_MODEL)),
                  full((D_MODEL, D_MODEL)),
                  mod_spec(2), mod_spec(3), mod_spec(4),
                  full((1, D_MODEL)),
                  full((ROUTE_ROWS, D_MODEL)),
                  full((ROUTE_ROWS, 1))],
        out_specs=[pl.BlockSpec((TM_PROJ, D_MODEL), lambda i: (i, 0)),
                   pl.BlockSpec((TM_PROJ, D_MODEL), lambda i: (i, 0)),
                   pl.BlockSpec((ROUTE_ROWS, TM_PROJ), lambda i: (0, i))],
        out_shape=[jax.ShapeDtypeStruct((N_TOK, D_MODEL), F32),
                   jax.ShapeDtypeStruct((N_TOK, D_MODEL), F32),
                   jax.ShapeDtypeStruct((ROUTE_ROWS, N_TOK), F32)],
        scratch_shapes=[pltpu.VMEM((TM_PROJ + HALO, POOL_WIDTH), F32)],
        compiler_params=_cparams(1),
        name="merge",
    )(x2, o_sb, proj, proj, proj, proj, poolw_bf, pool_scale, wsb_bf, wpool_bf, wout_bf,
      mods4, mods4, mods4, g2, wr_bf, br)


def _first_index_of(mask, idx, big):
    return jnp.min(jnp.where(mask, idx, big), axis=0, keepdims=True)


def _route_kernel(lt_ref, info_ref, slot_ref, cnt_ref, base_ref, info_scr):
    phase = pl.program_id(0)
    step = pl.program_id(1)
    tn = TN_ROUTE
    tile = pl.ds(pl.multiple_of(step * tn, tn), tn)
    row8 = lax.broadcasted_iota(jnp.int32, (8, tn), 0).astype(F32)
    erow = lax.broadcasted_iota(jnp.int32, (N_EXPERTS, tn), 0).astype(F32)

    @pl.when((phase == 0) & (step == 0))
    def _():
        base_ref[...] = jnp.zeros_like(base_ref)

    @pl.when(phase == 0)
    def _():
        gl = lt_ref[0:8, :]
        gmax = jnp.max(gl, axis=0, keepdims=True)
        gsel = _first_index_of(gl == gmax, row8, 8.0)
        p_group = 1.0 / jnp.sum(jnp.exp(gl - gmax), axis=0, keepdims=True)
        ig = lt_ref[8:16, :]
        for g in range(1, N_EXPERT_GROUPS):
            ig = jnp.where(gsel == float(g), lt_ref[8 + 8 * g:16 + 8 * g, :], ig)
        m1 = jnp.max(ig, axis=0, keepdims=True)
        i1 = _first_index_of(ig == m1, row8, 8.0)
        rest = jnp.where(row8 == i1, -jnp.inf, ig)
        m2 = jnp.max(rest, axis=0, keepdims=True)
        i2 = _first_index_of(rest == m2, row8, 8.0)
        e2 = jnp.exp(m2 - m1)
        w_first = p_group / (1.0 + e2)
        w_second = p_group * e2 / (1.0 + e2)
        e_first = gsel * EXPERTS_PER_GROUP + i1
        e_second = gsel * EXPERTS_PER_GROUP + i2
        hit_first = erow == e_first
        hit_second = erow == e_second
        onehot = jnp.where(hit_first | hit_second, 1.0, 0.0)
        r_i = lax.broadcasted_iota(jnp.int32, (tn, tn), 0)
        c_i = lax.broadcasted_iota(jnp.int32, (tn, tn), 1)
        tri = jnp.where(r_i < c_i, 1.0, 0.0).astype(BF16)
        rank = jnp.dot(onehot.astype(BF16), tri, preferred_element_type=F32) + base_ref[...]
        rank_first = jnp.sum(jnp.where(hit_first, rank, 0.0), axis=0, keepdims=True)
        rank_second = jnp.sum(jnp.where(hit_second, rank, 0.0), axis=0, keepdims=True)
        base_ref[...] = base_ref[...] + jnp.sum(onehot, axis=1, keepdims=True)
        info = jnp.zeros((8, tn), F32)
        for r, val in enumerate((e_first, e_second, rank_first, rank_second, w_first, w_second)):
            info = jnp.where(row8 == float(r), val, info)
        info_scr[:, tile] = info

    @pl.when(phase == 1)
    def _():
        counts = base_ref[...]
        padded = jnp.floor((counts + (TB_EXP - 1.0)) * (1.0 / TB_EXP)) * TB_EXP
        r_e = lax.broadcasted_iota(jnp.int32, (N_EXPERTS, N_EXPERTS), 0)
        c_e = lax.broadcasted_iota(jnp.int32, (N_EXPERTS, N_EXPERTS), 1)
        padded_row = jnp.sum(jnp.where(r_e == c_e, padded, 0.0), axis=0, keepdims=True)
        pad_start = jnp.sum(jnp.where(c_e < r_e, padded_row, 0.0), axis=1, keepdims=True)
        info = info_scr[:, tile]
        slot_first = jnp.sum(jnp.where(erow == info[0:1, :], pad_start, 0.0), axis=0, keepdims=True) + info[2:3, :]
        slot_second = jnp.sum(jnp.where(erow == info[1:2, :], pad_start, 0.0), axis=0, keepdims=True) + info[3:4, :]
        slots = jnp.where(row8 == 0.0, slot_first, jnp.where(row8 == 1.0, slot_second, 0.0))
        slot_ref[...] = slots.astype(jnp.int32)
        info_ref[...] = info
        cnt_ref[...] = counts


def _route(logits_t):
    n_steps = N_TOK // TN_ROUTE
    return pl.pallas_call(
        _route_kernel,
        grid=(2, n_steps),
        in_specs=[pl.BlockSpec((ROUTE_ROWS, TN_ROUTE), lambda p, i: (0, jnp.where(p == 0, i, n_steps - 1)))],
        out_specs=[pl.BlockSpec((8, TN_ROUTE), lambda p, i: (0, i * p)),
                   pl.BlockSpec((8, TN_ROUTE), lambda p, i: (0, i * p)),
                   pl.BlockSpec((N_EXPERTS, 1), lambda p, i: (0, 0))],
        out_shape=[jax.ShapeDtypeStruct((8, N_TOK), F32),
                   jax.ShapeDtypeStruct((8, N_TOK), jnp.int32),
                   jax.ShapeDtypeStruct((N_EXPERTS, 1), F32)],
        scratch_shapes=[pltpu.VMEM((N_EXPERTS, 1), F32),
                        pltpu.VMEM((8, N_TOK), F32)],
        compiler_params=_cparams(2),
        name="route",
    )(logits_t)


def _dispatch_kernel(s0_ref, s1_ref, h2_ref, xs_in_hbm, xs_hbm, sem):
    del xs_in_hbm

    def row_copy(j, slot):
        return pltpu.make_async_copy(h2_ref.at[pl.ds(j, 1), :], xs_hbm.at[pl.ds(slot, 1), :], sem)

    def issue(j, c):
        row_copy(j, s0_ref[j]).start()
        row_copy(j, s1_ref[j]).start()
        return c

    lax.fori_loop(0, TT_DISP, issue, 0, unroll=8)

    def drain(j, c):
        row_copy(j, s0_ref[j]).wait()
        row_copy(j, s1_ref[j]).wait()
        return c

    lax.fori_loop(0, TT_DISP, drain, 0, unroll=8)


def _dispatch(slot0, slot1, h2):
    smem = lambda: pl.BlockSpec((TT_DISP,), lambda i: (i,), memory_space=pltpu.SMEM)
    xs_zero = jnp.zeros((N_SLOTS, D_MODEL), F32)
    return pl.pallas_call(
        _dispatch_kernel,
        grid=(N_TOK // TT_DISP,),
        in_specs=[smem(), smem(),
                  pl.BlockSpec((TT_DISP, D_MODEL), lambda i: (i, 0)),
                  pl.BlockSpec(memory_space=pl.ANY)],
        out_specs=pl.BlockSpec(memory_space=pl.ANY),
        out_shape=jax.ShapeDtypeStruct((N_SLOTS, D_MODEL), F32),
        scratch_shapes=[pltpu.SemaphoreType.DMA(())],
        input_output_aliases={3: 0},
        compiler_params=_cparams(1),
        name="dispatch",
    )(slot0, slot1, h2, xs_zero)


def _experts_kernel(be_ref, nused_ref, xs_ref, wg_ref, wu_ref, wd_ref, ys_ref):
    used = pl.program_id(0) < nused_ref[0]

    @pl.when(used)
    def _():
        x = xs_ref[...].astype(BF16)
        g = jnp.dot(x, wg_ref[...], preferred_element_type=F32)
        u = jnp.dot(x, wu_ref[...], preferred_element_type=F32)
        mid = (g * jax.nn.sigmoid(g) * u).astype(BF16)
        ys_ref[...] = jnp.dot(mid, wd_ref[...], preferred_element_type=F32)

    @pl.when(jnp.logical_not(used))
    def _():
        ys_ref[...] = jnp.zeros_like(ys_ref)


def _experts(block_e, n_used, xs, wg_bf, wu_bf, wd_bf):
    blk = lambda i, be, nu: (jnp.minimum(i, nu[0] - 1), 0)
    wsel = lambda i, be, nu: (be[i], 0, 0)
    return pl.pallas_call(
        _experts_kernel,
        grid_spec=pltpu.PrefetchScalarGridSpec(
            num_scalar_prefetch=2,
            grid=(N_BLOCKS,),
            in_specs=[pl.BlockSpec((TB_EXP, D_MODEL), blk),
                      pl.BlockSpec((None, D_MODEL, D_EXPERT), wsel),
                      pl.BlockSpec((None, D_MODEL, D_EXPERT), wsel),
                      pl.BlockSpec((None, D_EXPERT, D_MODEL), wsel)],
            out_specs=pl.BlockSpec((TB_EXP, D_MODEL), lambda i, be, nu: (i, 0))),
        out_shape=jax.ShapeDtypeStruct((N_SLOTS, D_MODEL), F32),
        compiler_params=_cparams(1),
        name="experts",
    )(block_e, n_used, xs, wg_bf, wu_bf, wd_bf)


def _combine_kernel(s0_ref, s1_ref, ys_hbm, x1_ref, wts_ref, gate2_ref, gf_ref, o_ref,
                    y0_ref, y1_ref, sem):
    def issue(j, c):
        pltpu.make_async_copy(ys_hbm.at[pl.ds(s0_ref[j], 1), :], y0_ref.at[pl.ds(j, 1), :], sem).start()
        pltpu.make_async_copy(ys_hbm.at[pl.ds(s1_ref[j], 1), :], y1_ref.at[pl.ds(j, 1), :], sem).start()
        return c

    lax.fori_loop(0, TC_COMB, issue, 0, unroll=8)

    def drain(j, c):
        pltpu.make_async_copy(ys_hbm.at[pl.ds(s0_ref[j], 1), :], y0_ref.at[pl.ds(j, 1), :], sem).wait()
        pltpu.make_async_copy(ys_hbm.at[pl.ds(s1_ref[j], 1), :], y1_ref.at[pl.ds(j, 1), :], sem).wait()
        return c

    lax.fori_loop(0, TC_COMB, drain, 0, unroll=8)
    wts = wts_ref[...]
    moe = wts[:, 4:5] * y0_ref[...] + wts[:, 5:6] * y1_ref[...]
    x2 = x1_ref[...] + gate2_ref[...] * moe
    ms = jnp.mean(x2 * x2, axis=-1, keepdims=True)
    o_ref[...] = x2 * lax.rsqrt(ms + EPS) * gf_ref[...]


def _combine(slot0, slot1, ys, x1, wts_rows, mods4, gf):
    tiles_per_seq = SEQ // TC_COMB
    smem = lambda: pl.BlockSpec((TC_COMB,), lambda i: (i,), memory_space=pltpu.SMEM)
    return pl.pallas_call(
        _combine_kernel,
        grid=(N_TOK // TC_COMB,),
        in_specs=[smem(), smem(),
                  pl.BlockSpec(memory_space=pl.ANY),
                  pl.BlockSpec((TC_COMB, D_MODEL), lambda i: (i, 0)),
                  pl.BlockSpec((TC_COMB, 8), lambda i: (i, 0)),
                  pl.BlockSpec((None, None, 1, D_MODEL), lambda i: (i // tiles_per_seq, 5, 0, 0)),
                  pl.BlockSpec((1, D_MODEL), lambda i: (0, 0))],
        out_specs=pl.BlockSpec((TC_COMB, D_MODEL), lambda i: (i, 0)),
        out_shape=jax.ShapeDtypeStruct((N_TOK, D_MODEL), F32),
        scratch_shapes=[pltpu.VMEM((TC_COMB, D_MODEL), F32),
                        pltpu.VMEM((TC_COMB, D_MODEL), F32),
                        pltpu.SemaphoreType.DMA(())],
        compiler_params=_cparams(1),
        name="combine",
    )(slot0, slot1, ys, x1, wts_rows, mods4, gf)


def _layer(x2, mods4, norm1_g, norm2_g, w_in, pool_w, pool_scale, w_branch_sb, w_branch_pool,
           w_out, w_rg, b_rg, w_re, b_re, w_gate, w_up, w_down, norm_f_g):
    q_scale = jnp.where(jnp.arange(MIX_PROJ_WIDTH) < SB_WIDTH, HEAD_DIM ** -0.5, 1.0).astype(F32)
    w_in_bf = (w_in * q_scale[None, :]).astype(BF16)
    proj = _inproj(x2, norm1_g.reshape(1, D_MODEL), mods4, w_in_bf)
    o_sb = _attention(proj)

    wr = jnp.concatenate([w_rg.T, jnp.zeros((4, D_MODEL), F32),
                          w_re.transpose(0, 2, 1).reshape(N_EXPERTS, D_MODEL)], axis=0)
    br = jnp.concatenate([b_rg, jnp.full((4,), NEG_BIG, F32), b_re.reshape(N_EXPERTS)]).reshape(ROUTE_ROWS, 1)
    x1, h2, logits_t = _merge(
        x2, o_sb, proj, mods4, norm2_g.reshape(1, D_MODEL), pool_w.astype(BF16),
        pool_scale.reshape(1, POOL_WIDTH), w_branch_sb.astype(BF16), w_branch_pool.astype(BF16),
        w_out.astype(BF16), wr.astype(BF16), br)

    info, slots, counts = _route(logits_t)
    slot0, slot1 = slots[0], slots[1]
    counts = counts.reshape(N_EXPERTS).astype(jnp.int32)
    pad_end = jnp.cumsum(((counts + TB_EXP - 1) // TB_EXP) * TB_EXP)
    block_first = jnp.arange(N_BLOCKS, dtype=jnp.int32) * TB_EXP
    block_e = jnp.minimum(jnp.sum((pad_end[None, :] <= block_first[:, None]).astype(jnp.int32), axis=1),
                          N_EXPERTS - 1).astype(jnp.int32)
    n_used = (pad_end[-1:] // TB_EXP).astype(jnp.int32)

    xs = _dispatch(slot0, slot1, h2)
    ys = _experts(block_e, n_used, xs, w_gate.astype(BF16), w_up.astype(BF16), w_down.astype(BF16))
    return _combine(slot0, slot1, ys, x1, info.T, mods4, norm_f_g.reshape(1, D_MODEL))


def kernel(x, c, norm1_g, norm2_g, w_ada, b_ada, w_in, pool_w, pool_scale, w_branch_sb, w_branch_pool, w_out, w_route_group, b_route_group, w_route_expert, b_route_expert, w_exp_gate, w_exp_up, w_exp_down, norm_f_g):
    assert x.shape == (BATCH, SEQ, D_MODEL) and w_ada.shape[0] == 1
    mods = _ada(c, w_ada[0], b_ada[0])
    mods4 = mods.reshape(BATCH, 6, 1, D_MODEL)
    out = _layer(x.reshape(N_TOK, D_MODEL), mods4, norm1_g[0], norm2_g[0], w_in[0], pool_w[0],
                 pool_scale[0], w_branch_sb[0], w_branch_pool[0], w_out[0], w_route_group[0],
                 b_route_group[0], w_route_expert[0], b_route_expert[0], w_exp_gate[0], w_exp_up[0],
                 w_exp_down[0], norm_f_g)
    return out.reshape(BATCH, SEQ, D_MODEL)
```

```python
import functools

import jax
import jax.numpy as jnp
from jax import lax
from jax.experimental import pallas as pl
from jax.experimental.pallas import tpu as pltpu

F32 = jnp.float32
BF16 = jnp.bfloat16

D_MODEL = 1024
BATCH = 16
SEQ = 2048
N_TOK = BATCH * SEQ
HEAD_DIM = 64
SB_WIDTH = 512
POOL_WIDTH = 512
POOL_WINDOWS = (2, 4, 8, 16)
POOL_GROUP_DIM = 128
MIX_PROJ_WIDTH = 4096
N_EXPERT_GROUPS = 4
EXPERTS_PER_GROUP = 8
N_EXPERTS = 32
D_EXPERT = 512
EPS = 1e-6

LANES = 128
SUBLANES = 8
ROW_TILE = (SUBLANES, D_MODEL // SUBLANES)
assert ROW_TILE[1] == LANES
HALO = 16
TM_PROJ = 512
TQ = 256
TK = 256
MXU_DIM = 256
ATTN_HEADS = 8
ATTN_WIDTH = ATTN_HEADS * HEAD_DIM
LOG2_E = 1.4426950408889634
TN_ROUTE = 512
TT_DISP = 1024
TB_EXP = 256
TC_COMB = 256
N_BLOCKS = (2 * N_TOK) // TB_EXP + N_EXPERTS
N_SLOTS = N_BLOCKS * TB_EXP
ROUTE_ROWS = 40
NEG_BIG = -1e30
VMEM_LIMIT = 56 * 1024 * 1024


def _cparams(n_axes):
    return pltpu.CompilerParams(dimension_semantics=("arbitrary",) * n_axes,
                                vmem_limit_bytes=VMEM_LIMIT)


def _ada_kernel(c_ref, w_ref, b_ref, o_ref):
    c = c_ref[...]
    ca = (c * jax.nn.sigmoid(c)).astype(BF16)
    o_ref[...] = jnp.dot(ca, w_ref[...].astype(BF16), preferred_element_type=F32) + b_ref[...]


def _ada(c, w_ada, b_ada):
    n_out = w_ada.shape[1]
    tn = 1024
    return pl.pallas_call(
        _ada_kernel,
        grid=(n_out // tn,),
        in_specs=[pl.BlockSpec((BATCH, D_MODEL), lambda j: (0, 0)),
                  pl.BlockSpec((D_MODEL, tn), lambda j: (0, j)),
                  pl.BlockSpec((1, tn), lambda j: (0, j))],
        out_specs=pl.BlockSpec((BATCH, tn), lambda j: (0, j)),
        out_shape=jax.ShapeDtypeStruct((BATCH, n_out), F32),
        compiler_params=_cparams(1),
        name="ada",
    )(c, w_ada, b_ada.reshape(1, n_out))


def _rms_mod(x, g, shift, scale):
    ms = jnp.mean(x * x, axis=-1, keepdims=True)
    y = x * lax.rsqrt(ms + EPS) * g
    return y * (1.0 + scale) + shift


def _inproj_kernel(x_ref, g_ref, sh_ref, sc_ref, w_ref, o_ref):
    h = _rms_mod(x_ref[...], g_ref[...], sh_ref[...], sc_ref[...]).astype(BF16)
    o_ref[:, 0:SB_WIDTH] = (jnp.dot(h, w_ref[:, 0:SB_WIDTH], preferred_element_type=F32) * LOG2_E).astype(BF16)
    for lo in range(SB_WIDTH, MIX_PROJ_WIDTH, SB_WIDTH):
        o_ref[:, lo:lo + SB_WIDTH] = jnp.dot(
            h, w_ref[:, lo:lo + SB_WIDTH], preferred_element_type=F32).astype(BF16)


def _inproj(x2, g1, mods4, w_in_bf):
    tiles_per_seq = SEQ // TM_PROJ
    mod_spec = lambda k: pl.BlockSpec((None, None, 1, D_MODEL),
                                      lambda i: (i // tiles_per_seq, k, 0, 0))
    return pl.pallas_call(
        _inproj_kernel,
        grid=(N_TOK // TM_PROJ,),
        in_specs=[pl.BlockSpec((TM_PROJ, D_MODEL), lambda i: (i, 0)),
                  pl.BlockSpec((1, D_MODEL), lambda i: (0, 0)),
                  mod_spec(0), mod_spec(1),
                  pl.BlockSpec((D_MODEL, MIX_PROJ_WIDTH), lambda i: (0, 0))],
        out_specs=pl.BlockSpec((TM_PROJ, MIX_PROJ_WIDTH), lambda i: (i, 0)),
        out_shape=jax.ShapeDtypeStruct((N_TOK, MIX_PROJ_WIDTH), BF16),
        compiler_params=_cparams(1),
        name="inproj",
    )(x2, g1, mods4, mods4, w_in_bf)


def _attn_kernel(q_ref, k_ref, v_ref, o_ref):
    qi = pl.program_id(2)
    head_q = lax.broadcasted_iota(jnp.int32, (TQ, MXU_DIM), 1) // HEAD_DIM
    head_k = lax.broadcasted_iota(jnp.int32, (TK, MXU_DIM), 1) // HEAD_DIM
    row = lax.broadcasted_iota(jnp.int32, (TQ, TK), 0)
    col = lax.broadcasted_iota(jnp.int32, (TQ, TK), 1)
    past = col < row
    upper = jnp.where(row > col, 1.0, 0.0).astype(BF16)
    heads_per_group = MXU_DIM // HEAD_DIM
    n_groups = ATTN_WIDTH // MXU_DIM
    q_heads = []
    for g in range(n_groups):
        qg = q_ref[:, g * MXU_DIM:(g + 1) * MXU_DIM]
        q_heads += [jnp.where(head_q == hh, qg, jnp.zeros_like(qg)) for hh in range(heads_per_group)]

    def block(kb, carry, diag):
        accs, rs = carry[:n_groups], carry[n_groups:]
        start = pl.multiple_of(kb * TK, TK)
        new_acc, new_r = [], []
        for g in range(n_groups):
            k = k_ref[pl.ds(start, TK), g * MXU_DIM:(g + 1) * MXU_DIM]
            v = v_ref[pl.ds(start, TK), g * MXU_DIM:(g + 1) * MXU_DIM]
            acc = accs[g]
            for hh in range(heads_per_group):
                r = rs[g * heads_per_group + hh]
                z2 = lax.dot_general(q_heads[g * heads_per_group + hh], k, (((1,), (1,)), ((), ())),
                                     preferred_element_type=F32)
                sp2 = jnp.maximum(z2, 0.0) + jnp.log(1.0 + jnp.exp2(-jnp.abs(z2))) * LOG2_E
                stay = jnp.where(past, sp2, 0.0) if diag else sp2
                cum = jnp.dot(stay.astype(BF16), upper, preferred_element_type=F32)
                a = jnp.exp2(z2 - (sp2 + cum + r))
                if diag:
                    a = jnp.where(past, a, 0.0)
                v_h = jnp.where(head_k == hh, v, jnp.zeros_like(v))
                acc = acc + jnp.dot(a.astype(BF16), v_h, preferred_element_type=F32)
                new_r.append(r + jnp.sum(stay, axis=-1, keepdims=True))
            new_acc.append(acc)
        return (*new_acc, *new_r)

    init = (jnp.zeros((TQ, MXU_DIM), F32),) * n_groups + (jnp.zeros((TQ, 1), F32),) * ATTN_HEADS
    carry = block(qi, init, True)
    carry = lax.fori_loop(0, qi, lambda j, c: block(qi - 1 - j, c, False), carry)
    for g in range(n_groups):
        o_ref[:, g * MXU_DIM:(g + 1) * MXU_DIM] = carry[g].astype(BF16)


def _attention(proj):
    n_q = SEQ // TQ
    n_hg = SB_WIDTH // ATTN_WIDTH
    return pl.pallas_call(
        _attn_kernel,
        grid=(BATCH, n_hg, n_q),
        in_specs=[pl.BlockSpec((TQ, ATTN_WIDTH), lambda b, h, i: (b * n_q + i, h)),
                  pl.BlockSpec((SEQ, ATTN_WIDTH), lambda b, h, i: (b, n_hg + h)),
                  pl.BlockSpec((SEQ, ATTN_WIDTH), lambda b, h, i: (b, 2 * n_hg + h))],
        out_specs=pl.BlockSpec((TQ, ATTN_WIDTH), lambda b, h, i: (b * n_q + i, h)),
        out_shape=jax.ShapeDtypeStruct((N_TOK, SB_WIDTH), BF16),
        compiler_params=_cparams(3),
        name="attn",
    )(proj, proj, proj)


def _merge_kernel(x_ref, osb_ref, p_ref, halo_ref, gsb_ref, gpool_ref,
                  poolw_ref, pscale_ref, wsb_ref, wpool_ref, wout_ref,
                  gate1_ref, shift2_ref, scale2_ref, g2_ref, wr_ref, br_ref,
                  x1_ref, h2_ref, lt_ref, pext_ref):
    i = pl.program_id(0)
    tiles_per_seq = SEQ // TM_PROJ
    t0 = (i % tiles_per_seq) * TM_PROJ
    halo = halo_ref[...].astype(F32)
    pext_ref[0:HALO, :] = jnp.where(t0 == 0, jnp.zeros_like(halo), halo)
    pext_ref[HALO:, :] = p_ref[...].astype(F32)
    pos = t0 + lax.broadcasted_iota(jnp.int32, (TM_PROJ, 1), 0)
    o_pool = []
    for g, w in enumerate(POOL_WINDOWS):
        cols = slice(g * POOL_GROUP_DIM, (g + 1) * POOL_GROUP_DIM)
        cur = pext_ref[HALO:, cols]
        win = cur
        for d in range(1, w):
            win = win + pext_ref[HALO - d:HALO - d + TM_PROJ, cols]
        count = jnp.minimum(pos + 1, w).astype(F32)
        pooled = win / count - cur
        og = jnp.dot(pooled.astype(BF16), poolw_ref[g], preferred_element_type=F32)
        o_pool.append(og * pscale_ref[:, cols])
    o_pool = jnp.concatenate(o_pool, axis=-1).astype(BF16)
    br_sb = jnp.dot(osb_ref[...], wsb_ref[...], preferred_element_type=F32)
    br_pool = jnp.dot(o_pool, wpool_ref[...], preferred_element_type=F32)
    merged = (jax.nn.sigmoid(gsb_ref[...].astype(F32)) * br_sb
              + jax.nn.sigmoid(gpool_ref[...].astype(F32)) * br_pool)
    mix = jnp.dot(merged.astype(BF16), wout_ref[...], preferred_element_type=F32)
    x1 = x_ref[...] + gate1_ref[...] * mix
    x1_ref[...] = x1
    h2 = _rms_mod(x1, g2_ref[...], shift2_ref[...], scale2_ref[...])
    h2_ref[...] = h2.reshape(TM_PROJ, *ROW_TILE)
    lt_ref[...] = lax.dot_general(wr_ref[...], h2.astype(BF16), (((1,), (1,)), ((), ())),
                                  preferred_element_type=F32) + br_ref[...]


def _merge(x2, o_sb, proj, mods4, g2, poolw_bf, pool_scale, wsb_bf, wpool_bf, wout_bf, wr_bf, br):
    tiles_per_seq = SEQ // TM_PROJ
    halo_blocks = TM_PROJ // HALO
    mod_spec = lambda k: pl.BlockSpec((None, None, 1, D_MODEL),
                                      lambda i: (i // tiles_per_seq, k, 0, 0))
    full = lambda shape: pl.BlockSpec(shape, lambda i: (0,) * len(shape))
    return pl.pallas_call(
        _merge_kernel,
        grid=(N_TOK // TM_PROJ,),
        in_specs=[pl.BlockSpec((TM_PROJ, D_MODEL), lambda i: (i, 0)),
                  pl.BlockSpec((TM_PROJ, SB_WIDTH), lambda i: (i, 0)),
                  pl.BlockSpec((TM_PROJ, POOL_WIDTH), lambda i: (i, 3)),
                  pl.BlockSpec((HALO, POOL_WIDTH),
                               lambda i: (jnp.maximum(i * halo_blocks - 1, 0), 3)),
                  pl.BlockSpec((TM_PROJ, D_MODEL), lambda i: (i, 2)),
                  pl.BlockSpec((TM_PROJ, D_MODEL), lambda i: (i, 3)),
                  full((len(POOL_WINDOWS), POOL_GROUP_DIM, POOL_GROUP_DIM)),
                  full((1, POOL_WIDTH)),
                  full((SB_WIDTH, D_MODEL)),
                  full((POOL_WIDTH, D_MODEL)),
                  full((D_MODEL, D_MODEL)),
                  mod_spec(2), mod_spec(3), mod_spec(4),
                  full((1, D_MODEL)),
                  full((ROUTE_ROWS, D_MODEL)),
                  full((ROUTE_ROWS, 1))],
        out_specs=[pl.BlockSpec((TM_PROJ, D_MODEL), lambda i: (i, 0)),
                   pl.BlockSpec((TM_PROJ, *ROW_TILE), lambda i: (i, 0, 0)),
                   pl.BlockSpec((ROUTE_ROWS, TM_PROJ), lambda i: (0, i))],
        out_shape=[jax.ShapeDtypeStruct((N_TOK, D_MODEL), F32),
                   jax.ShapeDtypeStruct((N_TOK, *ROW_TILE), F32),
                   jax.ShapeDtypeStruct((ROUTE_ROWS, N_TOK), F32)],
        scratch_shapes=[pltpu.VMEM((TM_PROJ + HALO, POOL_WIDTH), F32)],
        compiler_params=_cparams(1),
        name="merge",
    )(x2, o_sb, proj, proj, proj, proj, poolw_bf, pool_scale, wsb_bf, wpool_bf, wout_bf,
      mods4, mods4, mods4, g2, wr_bf, br)


def _first_index_of(mask, idx, big):
    return jnp.min(jnp.where(mask, idx, big), axis=0, keepdims=True)


def _route_kernel(lt_ref, info_ref, slot_ref, cnt_ref, base_ref, info_scr):
    phase = pl.program_id(0)
    step = pl.program_id(1)
    tn = TN_ROUTE
    tile = pl.ds(pl.multiple_of(step * tn, tn), tn)
    row8 = lax.broadcasted_iota(jnp.int32, (8, tn), 0).astype(F32)
    erow = lax.broadcasted_iota(jnp.int32, (N_EXPERTS, tn), 0).astype(F32)

    @pl.when((phase == 0) & (step == 0))
    def _():
        base_ref[...] = jnp.zeros_like(base_ref)

    @pl.when(phase == 0)
    def _():
        gl = lt_ref[0:8, :]
        gmax = jnp.max(gl, axis=0, keepdims=True)
        gsel = _first_index_of(gl == gmax, row8, 8.0)
        p_group = 1.0 / jnp.sum(jnp.exp(gl - gmax), axis=0, keepdims=True)
        ig = lt_ref[8:16, :]
        for g in range(1, N_EXPERT_GROUPS):
            ig = jnp.where(gsel == float(g), lt_ref[8 + 8 * g:16 + 8 * g, :], ig)
        m1 = jnp.max(ig, axis=0, keepdims=True)
        i1 = _first_index_of(ig == m1, row8, 8.0)
        rest = jnp.where(row8 == i1, -jnp.inf, ig)
        m2 = jnp.max(rest, axis=0, keepdims=True)
        i2 = _first_index_of(rest == m2, row8, 8.0)
        e2 = jnp.exp(m2 - m1)
        w_first = p_group / (1.0 + e2)
        w_second = p_group * e2 / (1.0 + e2)
        e_first = gsel * EXPERTS_PER_GROUP + i1
        e_second = gsel * EXPERTS_PER_GROUP + i2
        hit_first = erow == e_first
        hit_second = erow == e_second
        onehot = jnp.where(hit_first | hit_second, 1.0, 0.0)
        r_i = lax.broadcasted_iota(jnp.int32, (tn, tn), 0)
        c_i = lax.broadcasted_iota(jnp.int32, (tn, tn), 1)
        tri = jnp.where(r_i < c_i, 1.0, 0.0).astype(BF16)
        rank = jnp.dot(onehot.astype(BF16), tri, preferred_element_type=F32) + base_ref[...]
        rank_first = jnp.sum(jnp.where(hit_first, rank, 0.0), axis=0, keepdims=True)
        rank_second = jnp.sum(jnp.where(hit_second, rank, 0.0), axis=0, keepdims=True)
        base_ref[...] = base_ref[...] + jnp.sum(onehot, axis=1, keepdims=True)
        info = jnp.zeros((8, tn), F32)
        for r, val in enumerate((e_first, e_second, rank_first, rank_second, w_first, w_second)):
            info = jnp.where(row8 == float(r), val, info)
        info_scr[:, tile] = info

    @pl.when(phase == 1)
    def _():
        counts = base_ref[...]
        padded = jnp.floor((counts + (TB_EXP - 1.0)) * (1.0 / TB_EXP)) * TB_EXP
        r_e = lax.broadcasted_iota(jnp.int32, (N_EXPERTS, N_EXPERTS), 0)
        c_e = lax.broadcasted_iota(jnp.int32, (N_EXPERTS, N_EXPERTS), 1)
        padded_row = jnp.sum(jnp.where(r_e == c_e, padded, 0.0), axis=0, keepdims=True)
        pad_start = jnp.sum(jnp.where(c_e < r_e, padded_row, 0.0), axis=1, keepdims=True)
        info = info_scr[:, tile]
        slot_first = jnp.sum(jnp.where(erow == info[0:1, :], pad_start, 0.0), axis=0, keepdims=True) + info[2:3, :]
        slot_second = jnp.sum(jnp.where(erow == info[1:2, :], pad_start, 0.0), axis=0, keepdims=True) + info[3:4, :]
        slots = jnp.where(row8 == 0.0, slot_first, jnp.where(row8 == 1.0, slot_second, 0.0))
        slot_ref[...] = slots.astype(jnp.int32)
        info_ref[...] = info
        cnt_ref[...] = counts


def _route(logits_t):
    n_steps = N_TOK // TN_ROUTE
    return pl.pallas_call(
        _route_kernel,
        grid=(2, n_steps),
        in_specs=[pl.BlockSpec((ROUTE_ROWS, TN_ROUTE), lambda p, i: (0, jnp.where(p == 0, i, n_steps - 1)))],
        out_specs=[pl.BlockSpec((8, TN_ROUTE), lambda p, i: (0, i * p)),
                   pl.BlockSpec((8, TN_ROUTE), lambda p, i: (0, i * p)),
                   pl.BlockSpec((N_EXPERTS, 1), lambda p, i: (0, 0))],
        out_shape=[jax.ShapeDtypeStruct((8, N_TOK), F32),
                   jax.ShapeDtypeStruct((8, N_TOK), jnp.int32),
                   jax.ShapeDtypeStruct((N_EXPERTS, 1), F32)],
        scratch_shapes=[pltpu.VMEM((N_EXPERTS, 1), F32),
                        pltpu.VMEM((8, N_TOK), F32)],
        compiler_params=_cparams(2),
        name="route",
    )(logits_t)


def _dispatch_kernel(pend_ref, s0_ref, s1_ref, h2_ref, xs_hbm, zero_ref, sem, zero_sem):
    @pl.when(pl.program_id(0) == 0)
    def _():
        zero_ref[...] = jnp.zeros_like(zero_ref)
        clears = []
        for e in range(N_EXPERTS):
            end = pend_ref[e]
            has_block = end > (pend_ref[e - 1] if e > 0 else 0)
            start = pl.multiple_of(jnp.maximum(end - TB_EXP, 0), TB_EXP)
            clears.append((has_block, pltpu.make_async_copy(zero_ref, xs_hbm.at[pl.ds(start, TB_EXP)], zero_sem)))
        for t in range(N_BLOCKS - (2 * N_TOK) // TB_EXP):
            start = pl.multiple_of(jnp.minimum(pend_ref[N_EXPERTS - 1] + t * TB_EXP, N_SLOTS - TB_EXP), TB_EXP)
            unused = pend_ref[N_EXPERTS - 1] + t * TB_EXP < N_SLOTS
            clears.append((unused, pltpu.make_async_copy(zero_ref, xs_hbm.at[pl.ds(start, TB_EXP)], zero_sem)))
        for has_block, copy in clears:
            pl.when(has_block)(copy.start)
        for has_block, copy in clears:
            pl.when(has_block)(copy.wait)

    def row_copy(j, slot):
        return pltpu.make_async_copy(h2_ref.at[j], xs_hbm.at[slot], sem)

    def issue(j, c):
        row_copy(j, s0_ref[j]).start()
        row_copy(j, s1_ref[j]).start()
        return c

    lax.fori_loop(0, TT_DISP, issue, 0, unroll=8)

    def drain(j, c):
        row_copy(j, 0).wait()
        row_copy(j, 0).wait()
        return c

    lax.fori_loop(0, TT_DISP, drain, 0, unroll=8)


def _dispatch(pad_end, slot0, slot1, h2):
    smem = lambda: pl.BlockSpec((TT_DISP,), lambda i, pe: (i,), memory_space=pltpu.SMEM)
    return pl.pallas_call(
        _dispatch_kernel,
        grid_spec=pltpu.PrefetchScalarGridSpec(
            num_scalar_prefetch=1,
            grid=(N_TOK // TT_DISP,),
            in_specs=[smem(), smem(),
                      pl.BlockSpec((TT_DISP, *ROW_TILE), lambda i, pe: (i, 0, 0))],
            out_specs=pl.BlockSpec(memory_space=pl.ANY),
            scratch_shapes=[pltpu.VMEM((TB_EXP, *ROW_TILE), F32),
                            pltpu.SemaphoreType.DMA(()),
                            pltpu.SemaphoreType.DMA(())]),
        out_shape=jax.ShapeDtypeStruct((N_SLOTS, *ROW_TILE), F32),
        compiler_params=_cparams(1),
        name="dispatch",
    )(pad_end, slot0, slot1, h2)


def _experts_kernel(be_ref, nused_ref, xs_ref, wg_ref, wu_ref, wd_ref, ys_ref):
    used = pl.program_id(0) < nused_ref[0]

    @pl.when(used)
    def _():
        x = xs_ref[...].reshape(TB_EXP, D_MODEL).astype(BF16)
        g = jnp.dot(x, wg_ref[...], preferred_element_type=F32)
        u = jnp.dot(x, wu_ref[...], preferred_element_type=F32)
        mid = (g * jax.nn.sigmoid(g) * u).astype(BF16)
        y = jnp.dot(mid, wd_ref[...], preferred_element_type=F32)
        ys_ref[...] = y.reshape(TB_EXP, *ROW_TILE)

    @pl.when(jnp.logical_not(used))
    def _():
        ys_ref[...] = jnp.zeros_like(ys_ref)


def _experts(block_e, n_used, xs, wg_bf, wu_bf, wd_bf):
    blk = lambda i, be, nu: (jnp.minimum(i, nu[0] - 1), 0, 0)
    wsel = lambda i, be, nu: (be[i], 0, 0)
    return pl.pallas_call(
        _experts_kernel,
        grid_spec=pltpu.PrefetchScalarGridSpec(
            num_scalar_prefetch=2,
            grid=(N_BLOCKS,),
            in_specs=[pl.BlockSpec((TB_EXP, *ROW_TILE), blk),
                      pl.BlockSpec((None, D_MODEL, D_EXPERT), wsel),
                      pl.BlockSpec((None, D_MODEL, D_EXPERT), wsel),
                      pl.BlockSpec((None, D_EXPERT, D_MODEL), wsel)],
            out_specs=pl.BlockSpec((TB_EXP, *ROW_TILE), lambda i, be, nu: (i, 0, 0))),
        out_shape=jax.ShapeDtypeStruct((N_SLOTS, *ROW_TILE), F32),
        compiler_params=_cparams(1),
        name="experts",
    )(block_e, n_used, xs, wg_bf, wu_bf, wd_bf)


def _combine_kernel(s0_ref, s1_ref, s0_next_ref, s1_next_ref, ys_hbm, x1_ref, wts_ref, gate2_ref, gf_ref,
                    o_ref, y_ref, sem):
    i = pl.program_id(0)
    cur = i % 2

    def row_copy(slot, buf, pick, j):
        return pltpu.make_async_copy(ys_hbm.at[slot], y_ref.at[buf, pick, j], sem.at[buf])

    def issue(first_ref, second_ref, buf):
        def body(j, c):
            row_copy(first_ref[j], buf, 0, j).start()
            row_copy(second_ref[j], buf, 1, j).start()
            return c
        lax.fori_loop(0, TC_COMB, body, 0, unroll=8)

    @pl.when(i == 0)
    def _():
        issue(s0_ref, s1_ref, 0)

    @pl.when(i + 1 < pl.num_programs(0))
    def _():
        issue(s0_next_ref, s1_next_ref, 1 - cur)

    def drain(j, c):
        row_copy(0, cur, 0, j).wait()
        row_copy(0, cur, 1, j).wait()
        return c

    lax.fori_loop(0, TC_COMB, drain, 0, unroll=8)
    wts = wts_ref[...]
    y_first = y_ref[cur, 0].reshape(TC_COMB, D_MODEL)
    y_second = y_ref[cur, 1].reshape(TC_COMB, D_MODEL)
    moe = wts[:, 4:5] * y_first + wts[:, 5:6] * y_second
    x2 = x1_ref[...] + gate2_ref[...] * moe
    ms = jnp.mean(x2 * x2, axis=-1, keepdims=True)
    o_ref[...] = x2 * lax.rsqrt(ms + EPS) * gf_ref[...]


def _combine(slot0, slot1, ys, x1, wts_rows, mods4, gf):
    n_steps = N_TOK // TC_COMB
    tiles_per_seq = SEQ // TC_COMB
    smem = lambda: pl.BlockSpec((TC_COMB,), lambda i: (i,), memory_space=pltpu.SMEM)
    smem_next = lambda: pl.BlockSpec((TC_COMB,), lambda i: (jnp.minimum(i + 1, n_steps - 1),),
                                     memory_space=pltpu.SMEM)
    return pl.pallas_call(
        _combine_kernel,
        grid=(n_steps,),
        in_specs=[smem(), smem(), smem_next(), smem_next(),
                  pl.BlockSpec(memory_space=pl.ANY),
                  pl.BlockSpec((TC_COMB, D_MODEL), lambda i: (i, 0)),
                  pl.BlockSpec((TC_COMB, 8), lambda i: (i, 0)),
                  pl.BlockSpec((None, None, 1, D_MODEL), lambda i: (i // tiles_per_seq, 5, 0, 0)),
                  pl.BlockSpec((1, D_MODEL), lambda i: (0, 0))],
        out_specs=pl.BlockSpec((TC_COMB, D_MODEL), lambda i: (i, 0)),
        out_shape=jax.ShapeDtypeStruct((N_TOK, D_MODEL), F32),
        scratch_shapes=[pltpu.VMEM((2, 2, TC_COMB, *ROW_TILE), F32),
                        pltpu.SemaphoreType.DMA((2,))],
        compiler_params=_cparams(1),
        name="combine",
    )(slot0, slot1, slot0, slot1, ys, x1, wts_rows, mods4, gf)


def _layer(x2, mods4, norm1_g, norm2_g, w_in, pool_w, pool_scale, w_branch_sb, w_branch_pool,
           w_out, w_rg, b_rg, w_re, b_re, w_gate, w_up, w_down, norm_f_g):
    q_scale = jnp.where(jnp.arange(MIX_PROJ_WIDTH) < SB_WIDTH, HEAD_DIM ** -0.5, 1.0).astype(F32)
    w_in_bf = (w_in * q_scale[None, :]).astype(BF16)
    proj = _inproj(x2, norm1_g.reshape(1, D_MODEL), mods4, w_in_bf)
    o_sb = _attention(proj)

    wr = jnp.concatenate([w_rg.T, jnp.zeros((4, D_MODEL), F32),
                          w_re.transpose(0, 2, 1).reshape(N_EXPERTS, D_MODEL)], axis=0)
    br = jnp.concatenate([b_rg, jnp.full((4,), NEG_BIG, F32), b_re.reshape(N_EXPERTS)]).reshape(ROUTE_ROWS, 1)
    x1, h2, logits_t = _merge(
        x2, o_sb, proj, mods4, norm2_g.reshape(1, D_MODEL), pool_w.astype(BF16),
        pool_scale.reshape(1, POOL_WIDTH), w_branch_sb.astype(BF16), w_branch_pool.astype(BF16),
        w_out.astype(BF16), wr.astype(BF16), br)

    info, slots, counts = _route(logits_t)
    slot0, slot1 = slots[0], slots[1]
    counts = counts.reshape(N_EXPERTS).astype(jnp.int32)
    pad_end = jnp.cumsum(((counts + TB_EXP - 1) // TB_EXP) * TB_EXP)
    block_first = jnp.arange(N_BLOCKS, dtype=jnp.int32) * TB_EXP
    block_e = jnp.minimum(jnp.sum((pad_end[None, :] <= block_first[:, None]).astype(jnp.int32), axis=1),
                          N_EXPERTS - 1).astype(jnp.int32)
    n_used = (pad_end[-1:] // TB_EXP).astype(jnp.int32)

    xs = _dispatch(pad_end.astype(jnp.int32), slot0, slot1, h2)
    ys = _experts(block_e, n_used, xs, w_gate.astype(BF16), w_up.astype(BF16), w_down.astype(BF16))
    return _combine(slot0, slot1, ys, x1, info.T, mods4, norm_f_g.reshape(1, D_MODEL))


def kernel(x, c, norm1_g, norm2_g, w_ada, b_ada, w_in, pool_w, pool_scale, w_branch_sb, w_branch_pool, w_out, w_route_group, b_route_group, w_route_expert, b_route_expert, w_exp_gate, w_exp_up, w_exp_down, norm_f_g):
    assert x.shape == (BATCH, SEQ, D_MODEL) and w_ada.shape[0] == 1
    mods = _ada(c, w_ada[0], b_ada[0])
    mods4 = mods.reshape(BATCH, 6, 1, D_MODEL)
    out = _layer(x.reshape(N_TOK, D_MODEL), mods4, norm1_g[0], norm2_g[0], w_in[0], pool_w[0],
                 pool_scale[0], w_branch_sb[0], w_branch_pool[0], w_out[0], w_route_group[0],
                 b_route_group[0], w_route_expert[0], b_route_expert[0], w_exp_gate[0], w_exp_up[0],
                 w_exp_down[0], norm_f_g)
    return out.reshape(BATCH, SEQ, D_MODEL)
```

```python
import functools

import jax
import jax.numpy as jnp
from jax import lax
from jax.experimental import pallas as pl
from jax.experimental.pallas import tpu as pltpu

F32 = jnp.float32
BF16 = jnp.bfloat16

D_MODEL = 1024
BATCH = 16
SEQ = 2048
N_TOK = BATCH * SEQ
HEAD_DIM = 64
SB_WIDTH = 512
POOL_WIDTH = 512
POOL_WINDOWS = (2, 4, 8, 16)
POOL_GROUP_DIM = 128
MIX_PROJ_WIDTH = 4096
N_EXPERT_GROUPS = 4
EXPERTS_PER_GROUP = 8
N_EXPERTS = 32
D_EXPERT = 512
EPS = 1e-6

LANES = 128
SUBLANES = 8
ROW_TILE = (SUBLANES, D_MODEL // SUBLANES)
assert ROW_TILE[1] == LANES
HALO = 16
TM_PROJ = 512
TQ = 256
TK = 256
MXU_DIM = 256
ATTN_HEADS = 4
ATTN_WIDTH = ATTN_HEADS * HEAD_DIM
LOG2_E = 1.4426950408889634
TN_ROUTE = 512
TT_DISP = 1024
TB_EXP = 256
TC_COMB = 256
N_BLOCKS = (2 * N_TOK) // TB_EXP + N_EXPERTS
N_SLOTS = N_BLOCKS * TB_EXP
ROUTE_ROWS = 40
NEG_BIG = -1e30
VMEM_LIMIT = 56 * 1024 * 1024


def _cparams(n_axes):
    return pltpu.CompilerParams(dimension_semantics=("arbitrary",) * n_axes,
                                vmem_limit_bytes=VMEM_LIMIT)


def _ada_kernel(c_ref, w_ref, b_ref, o_ref):
    c = c_ref[...]
    ca = (c * jax.nn.sigmoid(c)).astype(BF16)
    o_ref[...] = jnp.dot(ca, w_ref[...].astype(BF16), preferred_element_type=F32) + b_ref[...]


def _ada(c, w_ada, b_ada):
    n_out = w_ada.shape[1]
    tn = 1024
    return pl.pallas_call(
        _ada_kernel,
        grid=(n_out // tn,),
        in_specs=[pl.BlockSpec((BATCH, D_MODEL), lambda j: (0, 0)),
                  pl.BlockSpec((D_MODEL, tn), lambda j: (0, j)),
                  pl.BlockSpec((1, tn), lambda j: (0, j))],
        out_specs=pl.BlockSpec((BATCH, tn), lambda j: (0, j)),
        out_shape=jax.ShapeDtypeStruct((BATCH, n_out), F32),
        compiler_params=_cparams(1),
        name="ada",
    )(c, w_ada, b_ada.reshape(1, n_out))


def _rms_mod(x, g, shift, scale):
    ms = jnp.mean(x * x, axis=-1, keepdims=True)
    y = x * lax.rsqrt(ms + EPS) * g
    return y * (1.0 + scale) + shift


def _inproj_kernel(x_ref, g_ref, sh_ref, sc_ref, w_ref, o_ref):
    h = _rms_mod(x_ref[...], g_ref[...], sh_ref[...], sc_ref[...]).astype(BF16)
    o_ref[:, 0:SB_WIDTH] = (jnp.dot(h, w_ref[:, 0:SB_WIDTH], preferred_element_type=F32) * LOG2_E).astype(BF16)
    for lo in range(SB_WIDTH, MIX_PROJ_WIDTH, SB_WIDTH):
        o_ref[:, lo:lo + SB_WIDTH] = jnp.dot(
            h, w_ref[:, lo:lo + SB_WIDTH], preferred_element_type=F32).astype(BF16)


def _inproj(x2, g1, mods4, w_in_bf):
    tiles_per_seq = SEQ // TM_PROJ
    mod_spec = lambda k: pl.BlockSpec((None, None, 1, D_MODEL),
                                      lambda i: (i // tiles_per_seq, k, 0, 0))
    return pl.pallas_call(
        _inproj_kernel,
        grid=(N_TOK // TM_PROJ,),
        in_specs=[pl.BlockSpec((TM_PROJ, D_MODEL), lambda i: (i, 0)),
                  pl.BlockSpec((1, D_MODEL), lambda i: (0, 0)),
                  mod_spec(0), mod_spec(1),
                  pl.BlockSpec((D_MODEL, MIX_PROJ_WIDTH), lambda i: (0, 0))],
        out_specs=pl.BlockSpec((TM_PROJ, MIX_PROJ_WIDTH), lambda i: (i, 0)),
        out_shape=jax.ShapeDtypeStruct((N_TOK, MIX_PROJ_WIDTH), BF16),
        compiler_params=_cparams(1),
        name="inproj",
    )(x2, g1, mods4, mods4, w_in_bf)


def _attn_kernel(q_ref, k_ref, v_ref, o_ref, z_a, z_b, a_a, a_b, acc_ref, r_ref):
    qi = pl.program_id(2)
    head_q = lax.broadcasted_iota(jnp.int32, (TQ, MXU_DIM), 1) // HEAD_DIM
    head_k = lax.broadcasted_iota(jnp.int32, (TK, MXU_DIM), 1) // HEAD_DIM
    row = lax.broadcasted_iota(jnp.int32, (TQ, TK), 0)
    col = lax.broadcasted_iota(jnp.int32, (TQ, TK), 1)
    past = col < row
    upper = jnp.where(row > col, 1.0, 0.0).astype(BF16)
    heads_per_group = MXU_DIM // HEAD_DIM
    n_groups = ATTN_WIDTH // MXU_DIM

    def group_cols(g):
        return slice(g * MXU_DIM, (g + 1) * MXU_DIM)

    def scores(kb, z_out):
        start = pl.multiple_of(kb * TK, TK)
        for g in range(n_groups):
            qg = q_ref[:, group_cols(g)]
            k = k_ref[pl.ds(start, TK), group_cols(g)]
            for hh in range(heads_per_group):
                q_h = jnp.where(head_q == hh, qg, jnp.zeros_like(qg))
                z_out[g * heads_per_group + hh] = lax.dot_general(
                    q_h, k, (((1,), (1,)), ((), ())), preferred_element_type=F32)

    def weights(z_in, a_out, diag):
        for h in range(ATTN_HEADS):
            z2 = z_in[h]
            sp2 = jnp.maximum(z2, 0.0) + jnp.log(1.0 + jnp.exp2(-jnp.abs(z2))) * LOG2_E
            stay = jnp.where(past, sp2, 0.0) if diag else sp2
            cum = jnp.dot(stay.astype(BF16), upper, preferred_element_type=F32)
            a = jnp.exp2(z2 - (sp2 + cum + r_ref[h]))
            if diag:
                a = jnp.where(past, a, 0.0)
            a_out[h] = a.astype(BF16)
            r_ref[h] = r_ref[h] + jnp.sum(stay, axis=-1, keepdims=True)

    def accumulate(kb, a_in):
        start = pl.multiple_of(kb * TK, TK)
        for g in range(n_groups):
            v = v_ref[pl.ds(start, TK), group_cols(g)]
            acc = acc_ref[g]
            for hh in range(heads_per_group):
                v_h = jnp.where(head_k == hh, v, jnp.zeros_like(v))
                acc = acc + jnp.dot(a_in[g * heads_per_group + hh], v_h, preferred_element_type=F32)
            acc_ref[g] = acc

    acc_ref[...] = jnp.zeros_like(acc_ref)
    r_ref[...] = jnp.zeros_like(r_ref)
    scores(qi, z_a)
    scores(jnp.maximum(qi - 1, 0), z_b)
    weights(z_a, a_a, True)

    def step(i, c):
        kb = qi - i
        nxt = jnp.maximum(kb - 1, 0)

        @pl.when(i % 2 == 1)
        def _():
            accumulate(kb + 1, a_a)
            scores(nxt, z_a)
            weights(z_b, a_b, False)

        @pl.when(i % 2 == 0)
        def _():
            accumulate(kb + 1, a_b)
            scores(nxt, z_b)
            weights(z_a, a_a, False)
        return c

    lax.fori_loop(1, qi + 1, step, 0)

    @pl.when(qi % 2 == 1)
    def _():
        accumulate(0, a_b)

    @pl.when(qi % 2 == 0)
    def _():
        accumulate(0, a_a)

    for g in range(n_groups):
        o_ref[:, group_cols(g)] = acc_ref[g].astype(BF16)


def _attention(proj):
    n_q = SEQ // TQ
    n_hg = SB_WIDTH // ATTN_WIDTH
    return pl.pallas_call(
        _attn_kernel,
        grid=(BATCH, n_hg, n_q),
        in_specs=[pl.BlockSpec((TQ, ATTN_WIDTH), lambda b, h, i: (b * n_q + i, h)),
                  pl.BlockSpec((SEQ, ATTN_WIDTH), lambda b, h, i: (b, n_hg + h)),
                  pl.BlockSpec((SEQ, ATTN_WIDTH), lambda b, h, i: (b, 2 * n_hg + h))],
        out_specs=pl.BlockSpec((TQ, ATTN_WIDTH), lambda b, h, i: (b * n_q + i, h)),
        out_shape=jax.ShapeDtypeStruct((N_TOK, SB_WIDTH), BF16),
        scratch_shapes=[pltpu.VMEM((ATTN_HEADS, TQ, TK), F32), pltpu.VMEM((ATTN_HEADS, TQ, TK), F32),
                        pltpu.VMEM((ATTN_HEADS, TQ, TK), BF16), pltpu.VMEM((ATTN_HEADS, TQ, TK), BF16),
                        pltpu.VMEM((ATTN_WIDTH // MXU_DIM, TQ, MXU_DIM), F32),
                        pltpu.VMEM((ATTN_HEADS, TQ, 1), F32)],
        compiler_params=_cparams(3),
        name="attn",
    )(proj, proj, proj)


def _merge_kernel(x_ref, osb_ref, p_ref, halo_ref, gsb_ref, gpool_ref,
                  poolw_ref, pscale_ref, wsb_ref, wpool_ref, wout_ref,
                  gate1_ref, shift2_ref, scale2_ref, g2_ref, wr_ref, br_ref,
                  x1_ref, h2_ref, lt_ref, pext_ref):
    i = pl.program_id(0)
    tiles_per_seq = SEQ // TM_PROJ
    t0 = (i % tiles_per_seq) * TM_PROJ
    halo = halo_ref[...].astype(F32)
    pext_ref[0:HALO, :] = jnp.where(t0 == 0, jnp.zeros_like(halo), halo)
    pext_ref[HALO:, :] = p_ref[...].astype(F32)
    pos = t0 + lax.broadcasted_iota(jnp.int32, (TM_PROJ, 1), 0)
    o_pool = []
    for g, w in enumerate(POOL_WINDOWS):
        cols = slice(g * POOL_GROUP_DIM, (g + 1) * POOL_GROUP_DIM)
        cur = pext_ref[HALO:, cols]
        win = cur
        for d in range(1, w):
            win = win + pext_ref[HALO - d:HALO - d + TM_PROJ, cols]
        count = jnp.minimum(pos + 1, w).astype(F32)
        pooled = win / count - cur
        og = jnp.dot(pooled.astype(BF16), poolw_ref[g], preferred_element_type=F32)
        o_pool.append(og * pscale_ref[:, cols])
    o_pool = jnp.concatenate(o_pool, axis=-1).astype(BF16)
    br_sb = jnp.dot(osb_ref[...], wsb_ref[...], preferred_element_type=F32)
    br_pool = jnp.dot(o_pool, wpool_ref[...], preferred_element_type=F32)
    merged = (jax.nn.sigmoid(gsb_ref[...].astype(F32)) * br_sb
              + jax.nn.sigmoid(gpool_ref[...].astype(F32)) * br_pool)
    mix = jnp.dot(merged.astype(BF16), wout_ref[...], preferred_element_type=F32)
    x1 = x_ref[...] + gate1_ref[...] * mix
    x1_ref[...] = x1
    h2 = _rms_mod(x1, g2_ref[...], shift2_ref[...], scale2_ref[...])
    h2_ref[...] = h2.reshape(TM_PROJ, *ROW_TILE)
    lt_ref[...] = lax.dot_general(wr_ref[...], h2.astype(BF16), (((1,), (1,)), ((), ())),
                                  preferred_element_type=F32) + br_ref[...]


def _merge(x2, o_sb, proj, mods4, g2, poolw_bf, pool_scale, wsb_bf, wpool_bf, wout_bf, wr_bf, br):
    tiles_per_seq = SEQ // TM_PROJ
    halo_blocks = TM_PROJ // HALO
    mod_spec = lambda k: pl.BlockSpec((None, None, 1, D_MODEL),
                                      lambda i: (i // tiles_per_seq, k, 0, 0))
    full = lambda shape: pl.BlockSpec(shape, lambda i: (0,) * len(shape))
    return pl.pallas_call(
        _merge_kernel,
        grid=(N_TOK // TM_PROJ,),
        in_specs=[pl.BlockSpec((TM_PROJ, D_MODEL), lambda i: (i, 0)),
                  pl.BlockSpec((TM_PROJ, SB_WIDTH), lambda i: (i, 0)),
                  pl.BlockSpec((TM_PROJ, POOL_WIDTH), lambda i: (i, 3)),
                  pl.BlockSpec((HALO, POOL_WIDTH),
                               lambda i: (jnp.maximum(i * halo_blocks - 1, 0), 3)),
                  pl.BlockSpec((TM_PROJ, D_MODEL), lambda i: (i, 2)),
                  pl.BlockSpec((TM_PROJ, D_MODEL), lambda i: (i, 3)),
                  full((len(POOL_WINDOWS), POOL_GROUP_DIM, POOL_GROUP_DIM)),
                  full((1, POOL_WIDTH)),
                  full((SB_WIDTH, D_MODEL)),
                  full((POOL_WIDTH, D_MODEL)),
                  full((D_MODEL, D_MODEL)),
                  mod_spec(2), mod_spec(3), mod_spec(4),
                  full((1, D_MODEL)),
                  full((ROUTE_ROWS, D_MODEL)),
                  full((ROUTE_ROWS, 1))],
        out_specs=[pl.BlockSpec((TM_PROJ, D_MODEL), lambda i: (i, 0)),
                   pl.BlockSpec((TM_PROJ, *ROW_TILE), lambda i: (i, 0, 0)),
                   pl.BlockSpec((ROUTE_ROWS, TM_PROJ), lambda i: (0, i))],
        out_shape=[jax.ShapeDtypeStruct((N_TOK, D_MODEL), F32),
                   jax.ShapeDtypeStruct((N_TOK, *ROW_TILE), F32),
                   jax.ShapeDtypeStruct((ROUTE_ROWS, N_TOK), F32)],
        scratch_shapes=[pltpu.VMEM((TM_PROJ + HALO, POOL_WIDTH), F32)],
        compiler_params=_cparams(1),
        name="merge",
    )(x2, o_sb, proj, proj, proj, proj, poolw_bf, pool_scale, wsb_bf, wpool_bf, wout_bf,
      mods4, mods4, mods4, g2, wr_bf, br)


def _first_index_of(mask, idx, big):
    return jnp.min(jnp.where(mask, idx, big), axis=0, keepdims=True)


def _route_kernel(lt_ref, info_ref, slot_ref, cnt_ref, base_ref, info_scr):
    phase = pl.program_id(0)
    step = pl.program_id(1)
    tn = TN_ROUTE
    tile = pl.ds(pl.multiple_of(step * tn, tn), tn)
    row8 = lax.broadcasted_iota(jnp.int32, (8, tn), 0).astype(F32)
    erow = lax.broadcasted_iota(jnp.int32, (N_EXPERTS, tn), 0).astype(F32)

    @pl.when((phase == 0) & (step == 0))
    def _():
        base_ref[...] = jnp.zeros_like(base_ref)

    @pl.when(phase == 0)
    def _():
        gl = lt_ref[0:8, :]
        gmax = jnp.max(gl, axis=0, keepdims=True)
        gsel = _first_index_of(gl == gmax, row8, 8.0)
        p_group = 1.0 / jnp.sum(jnp.exp(gl - gmax), axis=0, keepdims=True)
        ig = lt_ref[8:16, :]
        for g in range(1, N_EXPERT_GROUPS):
            ig = jnp.where(gsel == float(g), lt_ref[8 + 8 * g:16 + 8 * g, :], ig)
        m1 = jnp.max(ig, axis=0, keepdims=True)
        i1 = _first_index_of(ig == m1, row8, 8.0)
        rest = jnp.where(row8 == i1, -jnp.inf, ig)
        m2 = jnp.max(rest, axis=0, keepdims=True)
        i2 = _first_index_of(rest == m2, row8, 8.0)
        e2 = jnp.exp(m2 - m1)
        w_first = p_group / (1.0 + e2)
        w_second = p_group * e2 / (1.0 + e2)
        e_first = gsel * EXPERTS_PER_GROUP + i1
        e_second = gsel * EXPERTS_PER_GROUP + i2
        hit_first = erow == e_first
        hit_second = erow == e_second
        onehot = jnp.where(hit_first | hit_second, 1.0, 0.0)
        r_i = lax.broadcasted_iota(jnp.int32, (tn, tn), 0)
        c_i = lax.broadcasted_iota(jnp.int32, (tn, tn), 1)
        tri = jnp.where(r_i < c_i, 1.0, 0.0).astype(BF16)
        rank = jnp.dot(onehot.astype(BF16), tri, preferred_element_type=F32) + base_ref[...]
        rank_first = jnp.sum(jnp.where(hit_first, rank, 0.0), axis=0, keepdims=True)
        rank_second = jnp.sum(jnp.where(hit_second, rank, 0.0), axis=0, keepdims=True)
        base_ref[...] = base_ref[...] + jnp.sum(onehot, axis=1, keepdims=True)
        info = jnp.zeros((8, tn), F32)
        for r, val in enumerate((e_first, e_second, rank_first, rank_second, w_first, w_second)):
            info = jnp.where(row8 == float(r), val, info)
        info_scr[:, tile] = info

    @pl.when(phase == 1)
    def _():
        counts = base_ref[...]
        padded = jnp.floor((counts + (TB_EXP - 1.0)) * (1.0 / TB_EXP)) * TB_EXP
        r_e = lax.broadcasted_iota(jnp.int32, (N_EXPERTS, N_EXPERTS), 0)
        c_e = lax.broadcasted_iota(jnp.int32, (N_EXPERTS, N_EXPERTS), 1)
        padded_row = jnp.sum(jnp.where(r_e == c_e, padded, 0.0), axis=0, keepdims=True)
        pad_start = jnp.sum(jnp.where(c_e < r_e, padded_row, 0.0), axis=1, keepdims=True)
        info = info_scr[:, tile]
        slot_first = jnp.sum(jnp.where(erow == info[0:1, :], pad_start, 0.0), axis=0, keepdims=True) + info[2:3, :]
        slot_second = jnp.sum(jnp.where(erow == info[1:2, :], pad_start, 0.0), axis=0, keepdims=True) + info[3:4, :]
        slots = jnp.where(row8 == 0.0, slot_first, jnp.where(row8 == 1.0, slot_second, 0.0))
        slot_ref[...] = slots.astype(jnp.int32)
        info_ref[...] = info
        cnt_ref[...] = counts


def _route(logits_t):
    n_steps = N_TOK // TN_ROUTE
    return pl.pallas_call(
        _route_kernel,
        grid=(2, n_steps),
        in_specs=[pl.BlockSpec((ROUTE_ROWS, TN_ROUTE), lambda p, i: (0, jnp.where(p == 0, i, n_steps - 1)))],
        out_specs=[pl.BlockSpec((8, TN_ROUTE), lambda p, i: (0, i * p)),
                   pl.BlockSpec((8, TN_ROUTE), lambda p, i: (0, i * p)),
                   pl.BlockSpec((N_EXPERTS, 1), lambda p, i: (0, 0))],
        out_shape=[jax.ShapeDtypeStruct((8, N_TOK), F32),
                   jax.ShapeDtypeStruct((8, N_TOK), jnp.int32),
                   jax.ShapeDtypeStruct((N_EXPERTS, 1), F32)],
        scratch_shapes=[pltpu.VMEM((N_EXPERTS, 1), F32),
                        pltpu.VMEM((8, N_TOK), F32)],
        compiler_params=_cparams(2),
        name="route",
    )(logits_t)


def _dispatch_kernel(pend_ref, s0_ref, s1_ref, h2_ref, xs_hbm, zero_ref, sem, zero_sem):
    @pl.when(pl.program_id(0) == 0)
    def _():
        zero_ref[...] = jnp.zeros_like(zero_ref)
        clears = []
        for e in range(N_EXPERTS):
            end = pend_ref[e]
            has_block = end > (pend_ref[e - 1] if e > 0 else 0)
            start = pl.multiple_of(jnp.maximum(end - TB_EXP, 0), TB_EXP)
            clears.append((has_block, pltpu.make_async_copy(zero_ref, xs_hbm.at[pl.ds(start, TB_EXP)], zero_sem)))
        for t in range(N_BLOCKS - (2 * N_TOK) // TB_EXP):
            start = pl.multiple_of(jnp.minimum(pend_ref[N_EXPERTS - 1] + t * TB_EXP, N_SLOTS - TB_EXP), TB_EXP)
            unused = pend_ref[N_EXPERTS - 1] + t * TB_EXP < N_SLOTS
            clears.append((unused, pltpu.make_async_copy(zero_ref, xs_hbm.at[pl.ds(start, TB_EXP)], zero_sem)))
        for has_block, copy in clears:
            pl.when(has_block)(copy.start)
        for has_block, copy in clears:
            pl.when(has_block)(copy.wait)

    def row_copy(j, slot):
        return pltpu.make_async_copy(h2_ref.at[j], xs_hbm.at[slot], sem)

    def issue(j, c):
        row_copy(j, s0_ref[j]).start(priority=0)
        row_copy(j, s1_ref[j]).start(priority=1)
        return c

    lax.fori_loop(0, TT_DISP, issue, 0, unroll=8)

    def drain(j, c):
        row_copy(j, 0).wait()
        row_copy(j, 0).wait()
        return c

    lax.fori_loop(0, TT_DISP, drain, 0, unroll=8)


def _dispatch(pad_end, slot0, slot1, h2):
    smem = lambda: pl.BlockSpec((TT_DISP,), lambda i, pe: (i,), memory_space=pltpu.SMEM)
    return pl.pallas_call(
        _dispatch_kernel,
        grid_spec=pltpu.PrefetchScalarGridSpec(
            num_scalar_prefetch=1,
            grid=(N_TOK // TT_DISP,),
            in_specs=[smem(), smem(),
                      pl.BlockSpec((TT_DISP, *ROW_TILE), lambda i, pe: (i, 0, 0))],
            out_specs=pl.BlockSpec(memory_space=pl.ANY),
            scratch_shapes=[pltpu.VMEM((TB_EXP, *ROW_TILE), F32),
                            pltpu.SemaphoreType.DMA(()),
                            pltpu.SemaphoreType.DMA(())]),
        out_shape=jax.ShapeDtypeStruct((N_SLOTS, *ROW_TILE), F32),
        compiler_params=_cparams(1),
        name="dispatch",
    )(pad_end, slot0, slot1, h2)


def _experts_kernel(be_ref, nused_ref, xs_ref, wg_ref, wu_ref, wd_ref, ys_ref):
    used = pl.program_id(0) < nused_ref[0]

    @pl.when(used)
    def _():
        x = xs_ref[...].reshape(TB_EXP, D_MODEL).astype(BF16)
        g = jnp.dot(x, wg_ref[...], preferred_element_type=F32)
        u = jnp.dot(x, wu_ref[...], preferred_element_type=F32)
        mid = (g * jax.nn.sigmoid(g) * u).astype(BF16)
        y = jnp.dot(mid, wd_ref[...], preferred_element_type=F32)
        ys_ref[...] = y.reshape(TB_EXP, *ROW_TILE)

    @pl.when(jnp.logical_not(used))
    def _():
        ys_ref[...] = jnp.zeros_like(ys_ref)


def _experts(block_e, n_used, xs, wg_bf, wu_bf, wd_bf):
    blk = lambda i, be, nu: (jnp.minimum(i, nu[0] - 1), 0, 0)
    wsel = lambda i, be, nu: (be[i], 0, 0)
    return pl.pallas_call(
        _experts_kernel,
        grid_spec=pltpu.PrefetchScalarGridSpec(
            num_scalar_prefetch=2,
            grid=(N_BLOCKS,),
            in_specs=[pl.BlockSpec((TB_EXP, *ROW_TILE), blk),
                      pl.BlockSpec((None, D_MODEL, D_EXPERT), wsel),
                      pl.BlockSpec((None, D_MODEL, D_EXPERT), wsel),
                      pl.BlockSpec((None, D_EXPERT, D_MODEL), wsel)],
            out_specs=pl.BlockSpec((TB_EXP, *ROW_TILE), lambda i, be, nu: (i, 0, 0))),
        out_shape=jax.ShapeDtypeStruct((N_SLOTS, *ROW_TILE), F32),
        compiler_params=_cparams(1),
        name="experts",
    )(block_e, n_used, xs, wg_bf, wu_bf, wd_bf)


def _combine_kernel(s0_ref, s1_ref, s0_next_ref, s1_next_ref, ys_hbm, x1_ref, wts_ref, gate2_ref, gf_ref,
                    o_ref, y_ref, sem):
    i = pl.program_id(0)
    cur = i % 2

    def row_copy(slot, buf, pick, j):
        return pltpu.make_async_copy(ys_hbm.at[slot], y_ref.at[buf, pick, j], sem.at[buf])

    def issue(first_ref, second_ref, buf):
        def body(j, c):
            row_copy(first_ref[j], buf, 0, j).start(priority=0)
            row_copy(second_ref[j], buf, 1, j).start(priority=1)
            return c
        lax.fori_loop(0, TC_COMB, body, 0, unroll=8)

    @pl.when(i == 0)
    def _():
        issue(s0_ref, s1_ref, 0)

    @pl.when(i + 1 < pl.num_programs(0))
    def _():
        issue(s0_next_ref, s1_next_ref, 1 - cur)

    def drain(j, c):
        row_copy(0, cur, 0, j).wait()
        row_copy(0, cur, 1, j).wait()
        return c

    lax.fori_loop(0, TC_COMB, drain, 0, unroll=8)
    wts = wts_ref[...]
    y_first = y_ref[cur, 0].reshape(TC_COMB, D_MODEL)
    y_second = y_ref[cur, 1].reshape(TC_COMB, D_MODEL)
    moe = wts[:, 4:5] * y_first + wts[:, 5:6] * y_second
    x2 = x1_ref[...] + gate2_ref[...] * moe
    ms = jnp.mean(x2 * x2, axis=-1, keepdims=True)
    o_ref[...] = x2 * lax.rsqrt(ms + EPS) * gf_ref[...]


def _combine(slot0, slot1, ys, x1, wts_rows, mods4, gf):
    n_steps = N_TOK // TC_COMB
    tiles_per_seq = SEQ // TC_COMB
    smem = lambda: pl.BlockSpec((TC_COMB,), lambda i: (i,), memory_space=pltpu.SMEM)
    smem_next = lambda: pl.BlockSpec((TC_COMB,), lambda i: (jnp.minimum(i + 1, n_steps - 1),),
                                     memory_space=pltpu.SMEM)
    return pl.pallas_call(
        _combine_kernel,
        grid=(n_steps,),
        in_specs=[smem(), smem(), smem_next(), smem_next(),
                  pl.BlockSpec(memory_space=pl.ANY),
                  pl.BlockSpec((TC_COMB, D_MODEL), lambda i: (i, 0)),
                  pl.BlockSpec((TC_COMB, 8), lambda i: (i, 0)),
                  pl.BlockSpec((None, None, 1, D_MODEL), lambda i: (i // tiles_per_seq, 5, 0, 0)),
                  pl.BlockSpec((1, D_MODEL), lambda i: (0, 0))],
        out_specs=pl.BlockSpec((TC_COMB, D_MODEL), lambda i: (i, 0)),
        out_shape=jax.ShapeDtypeStruct((N_TOK, D_MODEL), F32),
        scratch_shapes=[pltpu.VMEM((2, 2, TC_COMB, *ROW_TILE), F32),
                        pltpu.SemaphoreType.DMA((2,))],
        compiler_params=_cparams(1),
        name="combine",
    )(slot0, slot1, slot0, slot1, ys, x1, wts_rows, mods4, gf)


def _layer(x2, mods4, norm1_g, norm2_g, w_in, pool_w, pool_scale, w_branch_sb, w_branch_pool,
           w_out, w_rg, b_rg, w_re, b_re, w_gate, w_up, w_down, norm_f_g):
    q_scale = jnp.where(jnp.arange(MIX_PROJ_WIDTH) < SB_WIDTH, HEAD_DIM ** -0.5, 1.0).astype(F32)
    w_in_bf = (w_in * q_scale[None, :]).astype(BF16)
    proj = _inproj(x2, norm1_g.reshape(1, D_MODEL), mods4, w_in_bf)
    o_sb = _attention(proj)

    wr = jnp.concatenate([w_rg.T, jnp.zeros((4, D_MODEL), F32),
                          w_re.transpose(0, 2, 1).reshape(N_EXPERTS, D_MODEL)], axis=0)
    br = jnp.concatenate([b_rg, jnp.full((4,), NEG_BIG, F32), b_re.reshape(N_EXPERTS)]).reshape(ROUTE_ROWS, 1)
    x1, h2, logits_t = _merge(
        x2, o_sb, proj, mods4, norm2_g.reshape(1, D_MODEL), pool_w.astype(BF16),
        pool_scale.reshape(1, POOL_WIDTH), w_branch_sb.astype(BF16), w_branch_pool.astype(BF16),
        w_out.astype(BF16), wr.astype(BF16), br)

    info, slots, counts = _route(logits_t)
    slot0, slot1 = slots[0], slots[1]
    counts = counts.reshape(N_EXPERTS).astype(jnp.int32)
    pad_end = jnp.cumsum(((counts + TB_EXP - 1) // TB_EXP) * TB_EXP)
    block_first = jnp.arange(N_BLOCKS, dtype=jnp.int32) * TB_EXP
    block_e = jnp.minimum(jnp.sum((pad_end[None, :] <= block_first[:, None]).astype(jnp.int32), axis=1),
                          N_EXPERTS - 1).astype(jnp.int32)
    n_used = (pad_end[-1:] // TB_EXP).astype(jnp.int32)

    xs = _dispatch(pad_end.astype(jnp.int32), slot0, slot1, h2)
    ys = _experts(block_e, n_used, xs, w_gate.astype(BF16), w_up.astype(BF16), w_down.astype(BF16))
    return _combine(slot0, slot1, ys, x1, info.T, mods4, norm_f_g.reshape(1, D_MODEL))


def kernel(x, c, norm1_g, norm2_g, w_ada, b_ada, w_in, pool_w, pool_scale, w_branch_sb, w_branch_pool, w_out, w_route_group, b_route_group, w_route_expert, b_route_expert, w_exp_gate, w_exp_up, w_exp_down, norm_f_g):
    assert x.shape == (BATCH, SEQ, D_MODEL) and w_ada.shape[0] == 1
    mods = _ada(c, w_ada[0], b_ada[0])
    mods4 = mods.reshape(BATCH, 6, 1, D_MODEL)
    out = _layer(x.reshape(N_TOK, D_MODEL), mods4, norm1_g[0], norm2_g[0], w_in[0], pool_w[0],
                 pool_scale[0], w_branch_sb[0], w_branch_pool[0], w_out[0], w_route_group[0],
                 b_route_group[0], w_route_expert[0], b_route_expert[0], w_exp_gate[0], w_exp_up[0],
                 w_exp_down[0], norm_f_g)
    return out.reshape(BATCH, SEQ, D_MODEL)
```

```python
import functools

import jax
import jax.numpy as jnp
from jax import lax
from jax.experimental import pallas as pl
from jax.experimental.pallas import tpu as pltpu

F32 = jnp.float32
BF16 = jnp.bfloat16

D_MODEL = 1024
BATCH = 16
SEQ = 2048
N_TOK = BATCH * SEQ
HEAD_DIM = 64
SB_WIDTH = 512
POOL_WIDTH = 512
POOL_WINDOWS = (2, 4, 8, 16)
POOL_GROUP_DIM = 128
MIX_PROJ_WIDTH = 4096
N_EXPERT_GROUPS = 4
EXPERTS_PER_GROUP = 8
N_EXPERTS = 32
D_EXPERT = 512
EPS = 1e-6

LANES = 128
SUBLANES = 8
ROW_TILE = (SUBLANES, D_MODEL // SUBLANES)
assert ROW_TILE[1] == LANES
HALO = 16
TM_PROJ = 512
TQ = 256
TK = 256
MXU_DIM = 256
ATTN_HEADS = 4
ATTN_WIDTH = ATTN_HEADS * HEAD_DIM
LOG2_E = 1.4426950408889634
UNDERFLOW_LOG2 = 160.0
TN_ROUTE = 512
TT_DISP = 1024
TB_EXP = 256
TC_COMB = 256
N_BLOCKS = (2 * N_TOK) // TB_EXP + N_EXPERTS
N_SLOTS = N_BLOCKS * TB_EXP
ROUTE_ROWS = 40
NEG_BIG = -1e30
VMEM_LIMIT = 56 * 1024 * 1024


def _cparams(n_axes):
    return pltpu.CompilerParams(dimension_semantics=("arbitrary",) * n_axes,
                                vmem_limit_bytes=VMEM_LIMIT)


def _ada_kernel(c_ref, w_ref, b_ref, o_ref):
    c = c_ref[...]
    ca = (c * jax.nn.sigmoid(c)).astype(BF16)
    o_ref[...] = jnp.dot(ca, w_ref[...].astype(BF16), preferred_element_type=F32) + b_ref[...]


def _ada(c, w_ada, b_ada):
    n_out = w_ada.shape[1]
    tn = 1024
    return pl.pallas_call(
        _ada_kernel,
        grid=(n_out // tn,),
        in_specs=[pl.BlockSpec((BATCH, D_MODEL), lambda j: (0, 0)),
                  pl.BlockSpec((D_MODEL, tn), lambda j: (0, j)),
                  pl.BlockSpec((1, tn), lambda j: (0, j))],
        out_specs=pl.BlockSpec((BATCH, tn), lambda j: (0, j)),
        out_shape=jax.ShapeDtypeStruct((BATCH, n_out), F32),
        compiler_params=_cparams(1),
        name="ada",
    )(c, w_ada, b_ada.reshape(1, n_out))


def _rms_mod(x, g, shift, scale):
    ms = jnp.mean(x * x, axis=-1, keepdims=True)
    y = x * lax.rsqrt(ms + EPS) * g
    return y * (1.0 + scale) + shift


def _inproj_kernel(x_ref, g_ref, sh_ref, sc_ref, w_ref, o_ref):
    h = _rms_mod(x_ref[...], g_ref[...], sh_ref[...], sc_ref[...]).astype(BF16)
    o_ref[:, 0:SB_WIDTH] = (jnp.dot(h, w_ref[:, 0:SB_WIDTH], preferred_element_type=F32) * LOG2_E).astype(BF16)
    for lo in range(SB_WIDTH, MIX_PROJ_WIDTH, SB_WIDTH):
        o_ref[:, lo:lo + SB_WIDTH] = jnp.dot(
            h, w_ref[:, lo:lo + SB_WIDTH], preferred_element_type=F32).astype(BF16)


def _inproj(x2, g1, mods4, w_in_bf):
    tiles_per_seq = SEQ // TM_PROJ
    mod_spec = lambda k: pl.BlockSpec((None, None, 1, D_MODEL),
                                      lambda i: (i // tiles_per_seq, k, 0, 0))
    return pl.pallas_call(
        _inproj_kernel,
        grid=(N_TOK // TM_PROJ,),
        in_specs=[pl.BlockSpec((TM_PROJ, D_MODEL), lambda i: (i, 0)),
                  pl.BlockSpec((1, D_MODEL), lambda i: (0, 0)),
                  mod_spec(0), mod_spec(1),
                  pl.BlockSpec((D_MODEL, MIX_PROJ_WIDTH), lambda i: (0, 0))],
        out_specs=pl.BlockSpec((TM_PROJ, MIX_PROJ_WIDTH), lambda i: (i, 0)),
        out_shape=jax.ShapeDtypeStruct((N_TOK, MIX_PROJ_WIDTH), BF16),
        compiler_params=_cparams(1),
        name="inproj",
    )(x2, g1, mods4, mods4, w_in_bf)


def _attn_kernel(q_ref, k_ref, v_ref, o_ref, z_a, z_b, a_a, a_b, acc_ref, r_ref):
    qi = pl.program_id(2)
    head_q = lax.broadcasted_iota(jnp.int32, (TQ, MXU_DIM), 1) // HEAD_DIM
    head_k = lax.broadcasted_iota(jnp.int32, (TK, MXU_DIM), 1) // HEAD_DIM
    row = lax.broadcasted_iota(jnp.int32, (TQ, TK), 0)
    col = lax.broadcasted_iota(jnp.int32, (TQ, TK), 1)
    past = col < row
    upper = jnp.where(row > col, 1.0, 0.0).astype(BF16)
    heads_per_group = MXU_DIM // HEAD_DIM
    n_groups = ATTN_WIDTH // MXU_DIM

    def group_cols(g):
        return slice(g * MXU_DIM, (g + 1) * MXU_DIM)

    def scores(kb, z_out):
        start = pl.multiple_of(kb * TK, TK)
        for g in range(n_groups):
            qg = q_ref[:, group_cols(g)]
            k = k_ref[pl.ds(start, TK), group_cols(g)]
            for hh in range(heads_per_group):
                q_h = jnp.where(head_q == hh, qg, jnp.zeros_like(qg))
                z_out[g * heads_per_group + hh] = lax.dot_general(
                    q_h, k, (((1,), (1,)), ((), ())), preferred_element_type=F32)

    def weights(z_in, a_out, diag):
        for h in range(ATTN_HEADS):
            z2 = z_in[h]
            sp2 = jnp.maximum(z2, 0.0) + jnp.log(1.0 + jnp.exp2(-jnp.abs(z2))) * LOG2_E
            stay = jnp.where(past, sp2, 0.0) if diag else sp2
            cum = jnp.dot(stay.astype(BF16), upper, preferred_element_type=F32)
            a = jnp.exp2(z2 - (sp2 + cum + r_ref[h]))
            if diag:
                a = jnp.where(past, a, 0.0)
            a_out[h] = a.astype(BF16)
            r_ref[h] = r_ref[h] + jnp.sum(stay, axis=-1, keepdims=True)

    def accumulate(kb, a_in):
        start = pl.multiple_of(kb * TK, TK)
        for g in range(n_groups):
            v = v_ref[pl.ds(start, TK), group_cols(g)]
            acc = acc_ref[g]
            for hh in range(heads_per_group):
                v_h = jnp.where(head_k == hh, v, jnp.zeros_like(v))
                acc = acc + jnp.dot(a_in[g * heads_per_group + hh], v_h, preferred_element_type=F32)
            acc_ref[g] = acc

    acc_ref[...] = jnp.zeros_like(acc_ref)
    r_ref[...] = jnp.zeros_like(r_ref)
    scores(qi, z_a)
    scores(jnp.maximum(qi - 1, 0), z_b)
    weights(z_a, a_a, True)

    def step(carry):
        i, _ = carry
        kb = qi - i
        nxt = jnp.maximum(kb - 1, 0)

        @pl.when(i % 2 == 1)
        def _():
            accumulate(kb + 1, a_a)
            scores(nxt, z_a)
            weights(z_b, a_b, False)

        @pl.when(i % 2 == 0)
        def _():
            accumulate(kb + 1, a_b)
            scores(nxt, z_b)
            weights(z_a, a_a, False)
        return i + 1, jnp.min(r_ref[...]) >= UNDERFLOW_LOG2

    n_done, _ = lax.while_loop(lambda c: jnp.logical_and(c[0] <= qi, jnp.logical_not(c[1])),
                               step, (jnp.int32(1), False))
    last = n_done - 1

    @pl.when(last % 2 == 1)
    def _():
        accumulate(qi - last, a_b)

    @pl.when(last % 2 == 0)
    def _():
        accumulate(qi - last, a_a)

    for g in range(n_groups):
        o_ref[:, group_cols(g)] = acc_ref[g].astype(BF16)


def _attention(proj):
    n_q = SEQ // TQ
    n_hg = SB_WIDTH // ATTN_WIDTH
    return pl.pallas_call(
        _attn_kernel,
        grid=(BATCH, n_hg, n_q),
        in_specs=[pl.BlockSpec((TQ, ATTN_WIDTH), lambda b, h, i: (b * n_q + i, h)),
                  pl.BlockSpec((SEQ, ATTN_WIDTH), lambda b, h, i: (b, n_hg + h)),
                  pl.BlockSpec((SEQ, ATTN_WIDTH), lambda b, h, i: (b, 2 * n_hg + h))],
        out_specs=pl.BlockSpec((TQ, ATTN_WIDTH), lambda b, h, i: (b * n_q + i, h)),
        out_shape=jax.ShapeDtypeStruct((N_TOK, SB_WIDTH), BF16),
        scratch_shapes=[pltpu.VMEM((ATTN_HEADS, TQ, TK), F32), pltpu.VMEM((ATTN_HEADS, TQ, TK), F32),
                        pltpu.VMEM((ATTN_HEADS, TQ, TK), BF16), pltpu.VMEM((ATTN_HEADS, TQ, TK), BF16),
                        pltpu.VMEM((ATTN_WIDTH // MXU_DIM, TQ, MXU_DIM), F32),
                        pltpu.VMEM((ATTN_HEADS, TQ, 1), F32)],
        compiler_params=_cparams(3),
        name="attn",
    )(proj, proj, proj)


def _merge_kernel(x_ref, osb_ref, p_ref, halo_ref, gsb_ref, gpool_ref,
                  poolw_ref, pscale_ref, wsb_ref, wpool_ref, wout_ref,
                  gate1_ref, shift2_ref, scale2_ref, g2_ref, wr_ref, br_ref,
                  x1_ref, h2_ref, lt_ref, pext_ref):
    i = pl.program_id(0)
    tiles_per_seq = SEQ // TM_PROJ
    t0 = (i % tiles_per_seq) * TM_PROJ
    halo = halo_ref[...].astype(F32)
    pext_ref[0:HALO, :] = jnp.where(t0 == 0, jnp.zeros_like(halo), halo)
    pext_ref[HALO:, :] = p_ref[...].astype(F32)
    pos = t0 + lax.broadcasted_iota(jnp.int32, (TM_PROJ, 1), 0)
    o_pool = []
    for g, w in enumerate(POOL_WINDOWS):
        cols = slice(g * POOL_GROUP_DIM, (g + 1) * POOL_GROUP_DIM)
        cur = pext_ref[HALO:, cols]
        win = cur
        for d in range(1, w):
            win = win + pext_ref[HALO - d:HALO - d + TM_PROJ, cols]
        count = jnp.minimum(pos + 1, w).astype(F32)
        pooled = win / count - cur
        og = jnp.dot(pooled.astype(BF16), poolw_ref[g], preferred_element_type=F32)
        o_pool.append(og * pscale_ref[:, cols])
    o_pool = jnp.concatenate(o_pool, axis=-1).astype(BF16)
    br_sb = jnp.dot(osb_ref[...], wsb_ref[...], preferred_element_type=F32)
    br_pool = jnp.dot(o_pool, wpool_ref[...], preferred_element_type=F32)
    merged = (jax.nn.sigmoid(gsb_ref[...].astype(F32)) * br_sb
              + jax.nn.sigmoid(gpool_ref[...].astype(F32)) * br_pool)
    mix = jnp.dot(merged.astype(BF16), wout_ref[...], preferred_element_type=F32)
    x1 = x_ref[...] + gate1_ref[...] * mix
    x1_ref[...] = x1
    h2 = _rms_mod(x1, g2_ref[...], shift2_ref[...], scale2_ref[...])
    h2_ref[...] = h2.reshape(TM_PROJ, *ROW_TILE)
    lt_ref[...] = lax.dot_general(wr_ref[...], h2.astype(BF16), (((1,), (1,)), ((), ())),
                                  preferred_element_type=F32) + br_ref[...]


def _merge(x2, o_sb, proj, mods4, g2, poolw_bf, pool_scale, wsb_bf, wpool_bf, wout_bf, wr_bf, br):
    tiles_per_seq = SEQ // TM_PROJ
    halo_blocks = TM_PROJ // HALO
    mod_spec = lambda k: pl.BlockSpec((None, None, 1, D_MODEL),
                                      lambda i: (i // tiles_per_seq, k, 0, 0))
    full = lambda shape: pl.BlockSpec(shape, lambda i: (0,) * len(shape))
    return pl.pallas_call(
        _merge_kernel,
        grid=(N_TOK // TM_PROJ,),
        in_specs=[pl.BlockSpec((TM_PROJ, D_MODEL), lambda i: (i, 0)),
                  pl.BlockSpec((TM_PROJ, SB_WIDTH), lambda i: (i, 0)),
                  pl.BlockSpec((TM_PROJ, POOL_WIDTH), lambda i: (i, 3)),
                  pl.BlockSpec((HALO, POOL_WIDTH),
                               lambda i: (jnp.maximum(i * halo_blocks - 1, 0), 3)),
                  pl.BlockSpec((TM_PROJ, D_MODEL), lambda i: (i, 2)),
                  pl.BlockSpec((TM_PROJ, D_MODEL), lambda i: (i, 3)),
                  full((len(POOL_WINDOWS), POOL_GROUP_DIM, POOL_GROUP_DIM)),
                  full((1, POOL_WIDTH)),
                  full((SB_WIDTH, D_MODEL)),
                  full((POOL_WIDTH, D_MODEL)),
                  full((D_MODEL, D_MODEL)),
                  mod_spec(2), mod_spec(3), mod_spec(4),
                  full((1, D_MODEL)),
                  full((ROUTE_ROWS, D_MODEL)),
                  full((ROUTE_ROWS, 1))],
        out_specs=[pl.BlockSpec((TM_PROJ, D_MODEL), lambda i: (i, 0)),
                   pl.BlockSpec((TM_PROJ, *ROW_TILE), lambda i: (i, 0, 0)),
                   pl.BlockSpec((ROUTE_ROWS, TM_PROJ), lambda i: (0, i))],
        out_shape=[jax.ShapeDtypeStruct((N_TOK, D_MODEL), F32),
                   jax.ShapeDtypeStruct((N_TOK, *ROW_TILE), F32),
                   jax.ShapeDtypeStruct((ROUTE_ROWS, N_TOK), F32)],
        scratch_shapes=[pltpu.VMEM((TM_PROJ + HALO, POOL_WIDTH), F32)],
        compiler_params=_cparams(1),
        name="merge",
    )(x2, o_sb, proj, proj, proj, proj, poolw_bf, pool_scale, wsb_bf, wpool_bf, wout_bf,
      mods4, mods4, mods4, g2, wr_bf, br)


def _first_index_of(mask, idx, big):
    return jnp.min(jnp.where(mask, idx, big), axis=0, keepdims=True)


def _route_kernel(lt_ref, info_ref, slot_ref, cnt_ref, base_ref, info_scr):
    phase = pl.program_id(0)
    step = pl.program_id(1)
    tn = TN_ROUTE
    tile = pl.ds(pl.multiple_of(step * tn, tn), tn)
    row8 = lax.broadcasted_iota(jnp.int32, (8, tn), 0).astype(F32)
    erow = lax.broadcasted_iota(jnp.int32, (N_EXPERTS, tn), 0).astype(F32)

    @pl.when((phase == 0) & (step == 0))
    def _():
        base_ref[...] = jnp.zeros_like(base_ref)

    @pl.when(phase == 0)
    def _():
        gl = lt_ref[0:8, :]
        gmax = jnp.max(gl, axis=0, keepdims=True)
        gsel = _first_index_of(gl == gmax, row8, 8.0)
        p_group = 1.0 / jnp.sum(jnp.exp(gl - gmax), axis=0, keepdims=True)
        ig = lt_ref[8:16, :]
        for g in range(1, N_EXPERT_GROUPS):
            ig = jnp.where(gsel == float(g), lt_ref[8 + 8 * g:16 + 8 * g, :], ig)
        m1 = jnp.max(ig, axis=0, keepdims=True)
        i1 = _first_index_of(ig == m1, row8, 8.0)
        rest = jnp.where(row8 == i1, -jnp.inf, ig)
        m2 = jnp.max(rest, axis=0, keepdims=True)
        i2 = _first_index_of(rest == m2, row8, 8.0)
        e2 = jnp.exp(m2 - m1)
        w_first = p_group / (1.0 + e2)
        w_second = p_group * e2 / (1.0 + e2)
        e_first = gsel * EXPERTS_PER_GROUP + i1
        e_second = gsel * EXPERTS_PER_GROUP + i2
        hit_first = erow == e_first
        hit_second = erow == e_second
        onehot = jnp.where(hit_first | hit_second, 1.0, 0.0)
        r_i = lax.broadcasted_iota(jnp.int32, (tn, tn), 0)
        c_i = lax.broadcasted_iota(jnp.int32, (tn, tn), 1)
        tri = jnp.where(r_i < c_i, 1.0, 0.0).astype(BF16)
        rank = jnp.dot(onehot.astype(BF16), tri, preferred_element_type=F32) + base_ref[...]
        rank_first = jnp.sum(jnp.where(hit_first, rank, 0.0), axis=0, keepdims=True)
        rank_second = jnp.sum(jnp.where(hit_second, rank, 0.0), axis=0, keepdims=True)
        base_ref[...] = base_ref[...] + jnp.sum(onehot, axis=1, keepdims=True)
        info = jnp.zeros((8, tn), F32)
        for r, val in enumerate((e_first, e_second, rank_first, rank_second, w_first, w_second)):
            info = jnp.where(row8 == float(r), val, info)
        info_scr[:, tile] = info

    @pl.when(phase == 1)
    def _():
        counts = base_ref[...]
        padded = jnp.floor((counts + (TB_EXP - 1.0)) * (1.0 / TB_EXP)) * TB_EXP
        r_e = lax.broadcasted_iota(jnp.int32, (N_EXPERTS, N_EXPERTS), 0)
        c_e = lax.broadcasted_iota(jnp.int32, (N_EXPERTS, N_EXPERTS), 1)
        padded_row = jnp.sum(jnp.where(r_e == c_e, padded, 0.0), axis=0, keepdims=True)
        pad_start = jnp.sum(jnp.where(c_e < r_e, padded_row, 0.0), axis=1, keepdims=True)
        info = info_scr[:, tile]
        slot_first = jnp.sum(jnp.where(erow == info[0:1, :], pad_start, 0.0), axis=0, keepdims=True) + info[2:3, :]
        slot_second = jnp.sum(jnp.where(erow == info[1:2, :], pad_start, 0.0), axis=0, keepdims=True) + info[3:4, :]
        slots = jnp.where(row8 == 0.0, slot_first, jnp.where(row8 == 1.0, slot_second, 0.0))
        slot_ref[...] = slots.astype(jnp.int32)
        info_ref[...] = info
        cnt_ref[...] = counts


def _route(logits_t):
    n_steps = N_TOK // TN_ROUTE
    return pl.pallas_call(
        _route_kernel,
        grid=(2, n_steps),
        in_specs=[pl.BlockSpec((ROUTE_ROWS, TN_ROUTE), lambda p, i: (0, jnp.where(p == 0, i, n_steps - 1)))],
        out_specs=[pl.BlockSpec((8, TN_ROUTE), lambda p, i: (0, i * p)),
                   pl.BlockSpec((8, TN_ROUTE), lambda p, i: (0, i * p)),
                   pl.BlockSpec((N_EXPERTS, 1), lambda p, i: (0, 0))],
        out_shape=[jax.ShapeDtypeStruct((8, N_TOK), F32),
                   jax.ShapeDtypeStruct((8, N_TOK), jnp.int32),
                   jax.ShapeDtypeStruct((N_EXPERTS, 1), F32)],
        scratch_shapes=[pltpu.VMEM((N_EXPERTS, 1), F32),
                        pltpu.VMEM((8, N_TOK), F32)],
        compiler_params=_cparams(2),
        name="route",
    )(logits_t)


def _dispatch_kernel(pend_ref, s0_ref, s1_ref, h2_ref, xs_hbm, zero_ref, sem, zero_sem):
    @pl.when(pl.program_id(0) == 0)
    def _():
        zero_ref[...] = jnp.zeros_like(zero_ref)
        clears = []
        for e in range(N_EXPERTS):
            end = pend_ref[e]
            has_block = end > (pend_ref[e - 1] if e > 0 else 0)
            start = pl.multiple_of(jnp.maximum(end - TB_EXP, 0), TB_EXP)
            clears.append((has_block, pltpu.make_async_copy(zero_ref, xs_hbm.at[pl.ds(start, TB_EXP)], zero_sem)))
        for t in range(N_BLOCKS - (2 * N_TOK) // TB_EXP):
            start = pl.multiple_of(jnp.minimum(pend_ref[N_EXPERTS - 1] + t * TB_EXP, N_SLOTS - TB_EXP), TB_EXP)
            unused = pend_ref[N_EXPERTS - 1] + t * TB_EXP < N_SLOTS
            clears.append((unused, pltpu.make_async_copy(zero_ref, xs_hbm.at[pl.ds(start, TB_EXP)], zero_sem)))
        for has_block, copy in clears:
            pl.when(has_block)(copy.start)
        for has_block, copy in clears:
            pl.when(has_block)(copy.wait)

    def row_copy(j, slot):
        return pltpu.make_async_copy(h2_ref.at[j], xs_hbm.at[slot], sem)

    def issue(j, c):
        row_copy(j, s0_ref[j]).start(priority=0)
        row_copy(j, s1_ref[j]).start(priority=1)
        return c

    lax.fori_loop(0, TT_DISP, issue, 0, unroll=8)

    def drain(j, c):
        row_copy(j, 0).wait()
        row_copy(j, 0).wait()
        return c

    lax.fori_loop(0, TT_DISP, drain, 0, unroll=8)


def _dispatch(pad_end, slot0, slot1, h2):
    smem = lambda: pl.BlockSpec((TT_DISP,), lambda i, pe: (i,), memory_space=pltpu.SMEM)
    return pl.pallas_call(
        _dispatch_kernel,
        grid_spec=pltpu.PrefetchScalarGridSpec(
            num_scalar_prefetch=1,
            grid=(N_TOK // TT_DISP,),
            in_specs=[smem(), smem(),
                      pl.BlockSpec((TT_DISP, *ROW_TILE), lambda i, pe: (i, 0, 0))],
            out_specs=pl.BlockSpec(memory_space=pl.ANY),
            scratch_shapes=[pltpu.VMEM((TB_EXP, *ROW_TILE), F32),
                            pltpu.SemaphoreType.DMA(()),
                            pltpu.SemaphoreType.DMA(())]),
        out_shape=jax.ShapeDtypeStruct((N_SLOTS, *ROW_TILE), F32),
        compiler_params=_cparams(1),
        name="dispatch",
    )(pad_end, slot0, slot1, h2)


def _experts_kernel(be_ref, nused_ref, xs_ref, wg_ref, wu_ref, wd_ref, ys_ref):
    used = pl.program_id(0) < nused_ref[0]

    @pl.when(used)
    def _():
        x = xs_ref[...].reshape(TB_EXP, D_MODEL).astype(BF16)
        g = jnp.dot(x, wg_ref[...], preferred_element_type=F32)
        u = jnp.dot(x, wu_ref[...], preferred_element_type=F32)
        mid = (g * jax.nn.sigmoid(g) * u).astype(BF16)
        y = jnp.dot(mid, wd_ref[...], preferred_element_type=F32)
        ys_ref[...] = y.reshape(TB_EXP, *ROW_TILE)

    @pl.when(jnp.logical_not(used))
    def _():
        ys_ref[...] = jnp.zeros_like(ys_ref)


def _experts(block_e, n_used, xs, wg_bf, wu_bf, wd_bf):
    blk = lambda i, be, nu: (jnp.minimum(i, nu[0] - 1), 0, 0)
    wsel = lambda i, be, nu: (be[i], 0, 0)
    return pl.pallas_call(
        _experts_kernel,
        grid_spec=pltpu.PrefetchScalarGridSpec(
            num_scalar_prefetch=2,
            grid=(N_BLOCKS,),
            in_specs=[pl.BlockSpec((TB_EXP, *ROW_TILE), blk),
                      pl.BlockSpec((None, D_MODEL, D_EXPERT), wsel),
                      pl.BlockSpec((None, D_MODEL, D_EXPERT), wsel),
                      pl.BlockSpec((None, D_EXPERT, D_MODEL), wsel)],
            out_specs=pl.BlockSpec((TB_EXP, *ROW_TILE), lambda i, be, nu: (i, 0, 0))),
        out_shape=jax.ShapeDtypeStruct((N_SLOTS, *ROW_TILE), F32),
        compiler_params=_cparams(1),
        name="experts",
    )(block_e, n_used, xs, wg_bf, wu_bf, wd_bf)


def _combine_kernel(s0_ref, s1_ref, s0_next_ref, s1_next_ref, ys_hbm, x1_ref, wts_ref, gate2_ref, gf_ref,
                    o_ref, y_ref, sem):
    i = pl.program_id(0)
    cur = i % 2

    def row_copy(slot, buf, pick, j):
        return pltpu.make_async_copy(ys_hbm.at[slot], y_ref.at[buf, pick, j], sem.at[buf])

    def issue(first_ref, second_ref, buf):
        def body(j, c):
            row_copy(first_ref[j], buf, 0, j).start(priority=0)
            row_copy(second_ref[j], buf, 1, j).start(priority=1)
            return c
        lax.fori_loop(0, TC_COMB, body, 0, unroll=8)

    @pl.when(i == 0)
    def _():
        issue(s0_ref, s1_ref, 0)

    @pl.when(i + 1 < pl.num_programs(0))
    def _():
        issue(s0_next_ref, s1_next_ref, 1 - cur)

    def drain(j, c):
        row_copy(0, cur, 0, j).wait()
        row_copy(0, cur, 1, j).wait()
        return c

    lax.fori_loop(0, TC_COMB, drain, 0, unroll=8)
    wts = wts_ref[...]
    y_first = y_ref[cur, 0].reshape(TC_COMB, D_MODEL)
    y_second = y_ref[cur, 1].reshape(TC_COMB, D_MODEL)
    moe = wts[:, 4:5] * y_first + wts[:, 5:6] * y_second
    x2 = x1_ref[...] + gate2_ref[...] * moe
    ms = jnp.mean(x2 * x2, axis=-1, keepdims=True)
    o_ref[...] = x2 * lax.rsqrt(ms + EPS) * gf_ref[...]


def _combine(slot0, slot1, ys, x1, wts_rows, mods4, gf):
    n_steps = N_TOK // TC_COMB
    tiles_per_seq = SEQ // TC_COMB
    smem = lambda: pl.BlockSpec((TC_COMB,), lambda i: (i,), memory_space=pltpu.SMEM)
    smem_next = lambda: pl.BlockSpec((TC_COMB,), lambda i: (jnp.minimum(i + 1, n_steps - 1),),
                                     memory_space=pltpu.SMEM)
    return pl.pallas_call(
        _combine_kernel,
        grid=(n_steps,),
        in_specs=[smem(), smem(), smem_next(), smem_next(),
                  pl.BlockSpec(memory_space=pl.ANY),
                  pl.BlockSpec((TC_COMB, D_MODEL), lambda i: (i, 0)),
                  pl.BlockSpec((TC_COMB, 8), lambda i: (i, 0)),
                  pl.BlockSpec((None, None, 1, D_MODEL), lambda i: (i // tiles_per_seq, 5, 0, 0)),
                  pl.BlockSpec((1, D_MODEL), lambda i: (0, 0))],
        out_specs=pl.BlockSpec((TC_COMB, D_MODEL), lambda i: (i, 0)),
        out_shape=jax.ShapeDtypeStruct((N_TOK, D_MODEL), F32),
        scratch_shapes=[pltpu.VMEM((2, 2, TC_COMB, *ROW_TILE), F32),
                        pltpu.SemaphoreType.DMA((2,))],
        compiler_params=_cparams(1),
        name="combine",
    )(slot0, slot1, slot0, slot1, ys, x1, wts_rows, mods4, gf)


def _layer(x2, mods4, norm1_g, norm2_g, w_in, pool_w, pool_scale, w_branch_sb, w_branch_pool,
           w_out, w_rg, b_rg, w_re, b_re, w_gate, w_up, w_down, norm_f_g):
    q_scale = jnp.where(jnp.arange(MIX_PROJ_WIDTH) < SB_WIDTH, HEAD_DIM ** -0.5, 1.0).astype(F32)
    w_in_bf = (w_in * q_scale[None, :]).astype(BF16)
    proj = _inproj(x2, norm1_g.reshape(1, D_MODEL), mods4, w_in_bf)
    o_sb = _attention(proj)

    wr = jnp.concatenate([w_rg.T, jnp.zeros((4, D_MODEL), F32),
                          w_re.transpose(0, 2, 1).reshape(N_EXPERTS, D_MODEL)], axis=0)
    br = jnp.concatenate([b_rg, jnp.full((4,), NEG_BIG, F32), b_re.reshape(N_EXPERTS)]).reshape(ROUTE_ROWS, 1)
    x1, h2, logits_t = _merge(
        x2, o_sb, proj, mods4, norm2_g.reshape(1, D_MODEL), pool_w.astype(BF16),
        pool_scale.reshape(1, POOL_WIDTH), w_branch_sb.astype(BF16), w_branch_pool.astype(BF16),
        w_out.astype(BF16), wr.astype(BF16), br)

    info, slots, counts = _route(logits_t)
    slot0, slot1 = slots[0], slots[1]
    counts = counts.reshape(N_EXPERTS).astype(jnp.int32)
    pad_end = jnp.cumsum(((counts + TB_EXP - 1) // TB_EXP) * TB_EXP)
    block_first = jnp.arange(N_BLOCKS, dtype=jnp.int32) * TB_EXP
    block_e = jnp.minimum(jnp.sum((pad_end[None, :] <= block_first[:, None]).astype(jnp.int32), axis=1),
                          N_EXPERTS - 1).astype(jnp.int32)
    n_used = (pad_end[-1:] // TB_EXP).astype(jnp.int32)

    xs = _dispatch(pad_end.astype(jnp.int32), slot0, slot1, h2)
    ys = _experts(block_e, n_used, xs, w_gate.astype(BF16), w_up.astype(BF16), w_down.astype(BF16))
    return _combine(slot0, slot1, ys, x1, info.T, mods4, norm_f_g.reshape(1, D_MODEL))


def kernel(x, c, norm1_g, norm2_g, w_ada, b_ada, w_in, pool_w, pool_scale, w_branch_sb, w_branch_pool, w_out, w_route_group, b_route_group, w_route_expert, b_route_expert, w_exp_gate, w_exp_up, w_exp_down, norm_f_g):
    assert x.shape == (BATCH, SEQ, D_MODEL) and w_ada.shape[0] == 1
    mods = _ada(c, w_ada[0], b_ada[0])
    mods4 = mods.reshape(BATCH, 6, 1, D_MODEL)
    out = _layer(x.reshape(N_TOK, D_MODEL), mods4, norm1_g[0], norm2_g[0], w_in[0], pool_w[0],
                 pool_scale[0], w_branch_sb[0], w_branch_pool[0], w_out[0], w_route_group[0],
                 b_route_group[0], w_route_expert[0], b_route_expert[0], w_exp_gate[0], w_exp_up[0],
                 w_exp_down[0], norm_f_g)
    return out.reshape(BATCH, SEQ, D_MODEL)
```

```python
import functools

import jax
import jax.numpy as jnp
from jax import lax
from jax.experimental import pallas as pl
from jax.experimental.pallas import tpu as pltpu

F32 = jnp.float32
BF16 = jnp.bfloat16

D_MODEL = 1024
BATCH = 16
SEQ = 2048
N_TOK = BATCH * SEQ
HEAD_DIM = 64
SB_WIDTH = 512
POOL_WIDTH = 512
POOL_WINDOWS = (2, 4, 8, 16)
POOL_GROUP_DIM = 128
MIX_PROJ_WIDTH = 4096
N_EXPERT_GROUPS = 4
EXPERTS_PER_GROUP = 8
N_EXPERTS = 32
D_EXPERT = 512
EPS = 1e-6

LANES = 128
SUBLANES = 8
ROW_TILE = (SUBLANES, D_MODEL // SUBLANES)
assert ROW_TILE[1] == LANES
HALO = 16
TM_PROJ = 512
TQ = 256
TK = 256
MXU_DIM = 256
ATTN_HEADS = 8
ATTN_WIDTH = ATTN_HEADS * HEAD_DIM
LOG2_E = 1.4426950408889634
UNDERFLOW_LOG2 = 160.0
TN_ROUTE = 512
TT_DISP = 1024
TB_EXP = 512
TC_COMB = 256
N_BLOCKS = (2 * N_TOK) // TB_EXP + N_EXPERTS
N_SLOTS = N_BLOCKS * TB_EXP
ROUTE_ROWS = 40
NEG_BIG = -1e30
VMEM_LIMIT = 56 * 1024 * 1024


def _cparams(n_axes):
    return pltpu.CompilerParams(dimension_semantics=("arbitrary",) * n_axes,
                                vmem_limit_bytes=VMEM_LIMIT)


def _ada_kernel(c_ref, w_ref, b_ref, o_ref):
    c = c_ref[...]
    ca = (c * jax.nn.sigmoid(c)).astype(BF16)
    o_ref[...] = jnp.dot(ca, w_ref[...].astype(BF16), preferred_element_type=F32) + b_ref[...]


def _ada(c, w_ada, b_ada):
    n_out = w_ada.shape[1]
    tn = 1024
    return pl.pallas_call(
        _ada_kernel,
        grid=(n_out // tn,),
        in_specs=[pl.BlockSpec((BATCH, D_MODEL), lambda j: (0, 0)),
                  pl.BlockSpec((D_MODEL, tn), lambda j: (0, j)),
                  pl.BlockSpec((1, tn), lambda j: (0, j))],
        out_specs=pl.BlockSpec((BATCH, tn), lambda j: (0, j)),
        out_shape=jax.ShapeDtypeStruct((BATCH, n_out), F32),
        compiler_params=_cparams(1),
        name="ada",
    )(c, w_ada, b_ada.reshape(1, n_out))


def _rms_mod(x, g, shift, scale):
    ms = jnp.mean(x * x, axis=-1, keepdims=True)
    y = x * lax.rsqrt(ms + EPS) * g
    return y * (1.0 + scale) + shift


def _inproj_kernel(x_ref, g_ref, sh_ref, sc_ref, w_ref, o_ref):
    h = _rms_mod(x_ref[...], g_ref[...], sh_ref[...], sc_ref[...]).astype(BF16)
    o_ref[:, 0:SB_WIDTH] = (jnp.dot(h, w_ref[:, 0:SB_WIDTH], preferred_element_type=F32) * LOG2_E).astype(BF16)
    for lo in range(SB_WIDTH, MIX_PROJ_WIDTH, SB_WIDTH):
        o_ref[:, lo:lo + SB_WIDTH] = jnp.dot(
            h, w_ref[:, lo:lo + SB_WIDTH], preferred_element_type=F32).astype(BF16)


def _inproj(x2, g1, mods4, w_in_bf):
    tiles_per_seq = SEQ // TM_PROJ
    mod_spec = lambda k: pl.BlockSpec((None, None, 1, D_MODEL),
                                      lambda i: (i // tiles_per_seq, k, 0, 0))
    return pl.pallas_call(
        _inproj_kernel,
        grid=(N_TOK // TM_PROJ,),
        in_specs=[pl.BlockSpec((TM_PROJ, D_MODEL), lambda i: (i, 0)),
                  pl.BlockSpec((1, D_MODEL), lambda i: (0, 0)),
                  mod_spec(0), mod_spec(1),
                  pl.BlockSpec((D_MODEL, MIX_PROJ_WIDTH), lambda i: (0, 0))],
        out_specs=pl.BlockSpec((TM_PROJ, MIX_PROJ_WIDTH), lambda i: (i, 0)),
        out_shape=jax.ShapeDtypeStruct((N_TOK, MIX_PROJ_WIDTH), BF16),
        compiler_params=_cparams(1),
        name="inproj",
    )(x2, g1, mods4, mods4, w_in_bf)


def _attn_kernel(q_ref, k_ref, v_ref, o_ref, z_a, z_b, a_a, a_b, acc_ref, r_ref):
    qi = pl.program_id(2)
    head_q = lax.broadcasted_iota(jnp.int32, (TQ, MXU_DIM), 1) // HEAD_DIM
    head_k = lax.broadcasted_iota(jnp.int32, (TK, MXU_DIM), 1) // HEAD_DIM
    row = lax.broadcasted_iota(jnp.int32, (TQ, TK), 0)
    col = lax.broadcasted_iota(jnp.int32, (TQ, TK), 1)
    past = col < row
    upper = jnp.where(row > col, 1.0, 0.0).astype(BF16)
    heads_per_group = MXU_DIM // HEAD_DIM
    n_groups = ATTN_WIDTH // MXU_DIM

    def group_cols(g):
        return slice(g * MXU_DIM, (g + 1) * MXU_DIM)

    def scores(kb, z_out):
        start = pl.multiple_of(kb * TK, TK)
        for g in range(n_groups):
            qg = q_ref[:, group_cols(g)]
            k = k_ref[pl.ds(start, TK), group_cols(g)]
            for hh in range(heads_per_group):
                q_h = jnp.where(head_q == hh, qg, jnp.zeros_like(qg))
                z_out[g * heads_per_group + hh] = lax.dot_general(
                    q_h, k, (((1,), (1,)), ((), ())), preferred_element_type=F32)

    def weights(z_in, a_out, diag):
        for h in range(ATTN_HEADS):
            z2 = z_in[h]
            sp2 = jnp.maximum(z2, 0.0) + jnp.log(1.0 + jnp.exp2(-jnp.abs(z2))) * LOG2_E
            stay = jnp.where(past, sp2, 0.0) if diag else sp2
            cum = jnp.dot(stay.astype(BF16), upper, preferred_element_type=F32)
            a = jnp.exp2(z2 - (sp2 + cum + r_ref[h]))
            if diag:
                a = jnp.where(past, a, 0.0)
            a_out[h] = a.astype(BF16)
            r_ref[h] = r_ref[h] + jnp.sum(stay, axis=-1, keepdims=True)

    def accumulate(kb, a_in):
        start = pl.multiple_of(kb * TK, TK)
        for g in range(n_groups):
            v = v_ref[pl.ds(start, TK), group_cols(g)]
            acc = acc_ref[g]
            for hh in range(heads_per_group):
                v_h = jnp.where(head_k == hh, v, jnp.zeros_like(v))
                acc = acc + jnp.dot(a_in[g * heads_per_group + hh], v_h, preferred_element_type=F32)
            acc_ref[g] = acc

    acc_ref[...] = jnp.zeros_like(acc_ref)
    r_ref[...] = jnp.zeros_like(r_ref)
    scores(qi, z_a)
    scores(jnp.maximum(qi - 1, 0), z_b)
    weights(z_a, a_a, True)

    def step(carry):
        i, _ = carry
        kb = qi - i
        nxt = jnp.maximum(kb - 1, 0)

        @pl.when(i % 2 == 1)
        def _():
            accumulate(kb + 1, a_a)
            scores(nxt, z_a)
            weights(z_b, a_b, False)

        @pl.when(i % 2 == 0)
        def _():
            accumulate(kb + 1, a_b)
            scores(nxt, z_b)
            weights(z_a, a_a, False)
        return i + 1, jnp.min(r_ref[...]) >= UNDERFLOW_LOG2

    n_done, _ = lax.while_loop(lambda c: jnp.logical_and(c[0] <= qi, jnp.logical_not(c[1])),
                               step, (jnp.int32(1), False))
    last = n_done - 1

    @pl.when(last % 2 == 1)
    def _():
        accumulate(qi - last, a_b)

    @pl.when(last % 2 == 0)
    def _():
        accumulate(qi - last, a_a)

    for g in range(n_groups):
        o_ref[:, group_cols(g)] = acc_ref[g].astype(BF16)


def _attention(proj):
    n_q = SEQ // TQ
    n_hg = SB_WIDTH // ATTN_WIDTH
    return pl.pallas_call(
        _attn_kernel,
        grid=(BATCH, n_hg, n_q),
        in_specs=[pl.BlockSpec((TQ, ATTN_WIDTH), lambda b, h, i: (b * n_q + i, h)),
                  pl.BlockSpec((SEQ, ATTN_WIDTH), lambda b, h, i: (b, n_hg + h)),
                  pl.BlockSpec((SEQ, ATTN_WIDTH), lambda b, h, i: (b, 2 * n_hg + h))],
        out_specs=pl.BlockSpec((TQ, ATTN_WIDTH), lambda b, h, i: (b * n_q + i, h)),
        out_shape=jax.ShapeDtypeStruct((N_TOK, SB_WIDTH), BF16),
        scratch_shapes=[pltpu.VMEM((ATTN_HEADS, TQ, TK), F32), pltpu.VMEM((ATTN_HEADS, TQ, TK), F32),
                        pltpu.VMEM((ATTN_HEADS, TQ, TK), BF16), pltpu.VMEM((ATTN_HEADS, TQ, TK), BF16),
                        pltpu.VMEM((ATTN_WIDTH // MXU_DIM, TQ, MXU_DIM), F32),
                        pltpu.VMEM((ATTN_HEADS, TQ, 1), F32)],
        compiler_params=_cparams(3),
        name="attn",
    )(proj, proj, proj)


def _merge_kernel(x_ref, osb_ref, p_ref, halo_ref, gsb_ref, gpool_ref,
                  poolw_ref, pscale_ref, wsb_ref, wpool_ref, wout_ref,
                  gate1_ref, shift2_ref, scale2_ref, g2_ref, wr_ref, br_ref,
                  x1_ref, h2_ref, lt_ref, pext_ref):
    i = pl.program_id(0)
    tiles_per_seq = SEQ // TM_PROJ
    t0 = (i % tiles_per_seq) * TM_PROJ
    halo = halo_ref[...].astype(F32)
    pext_ref[0:HALO, :] = jnp.where(t0 == 0, jnp.zeros_like(halo), halo)
    pext_ref[HALO:, :] = p_ref[...].astype(F32)
    pos = t0 + lax.broadcasted_iota(jnp.int32, (TM_PROJ, 1), 0)
    o_pool = []
    for g, w in enumerate(POOL_WINDOWS):
        cols = slice(g * POOL_GROUP_DIM, (g + 1) * POOL_GROUP_DIM)
        cur = pext_ref[HALO:, cols]
        win = cur
        for d in range(1, w):
            win = win + pext_ref[HALO - d:HALO - d + TM_PROJ, cols]
        count = jnp.minimum(pos + 1, w).astype(F32)
        pooled = win / count - cur
        og = jnp.dot(pooled.astype(BF16), poolw_ref[g], preferred_element_type=F32)
        o_pool.append(og * pscale_ref[:, cols])
    o_pool = jnp.concatenate(o_pool, axis=-1).astype(BF16)
    br_sb = jnp.dot(osb_ref[...], wsb_ref[...], preferred_element_type=F32)
    br_pool = jnp.dot(o_pool, wpool_ref[...], preferred_element_type=F32)
    merged = (jax.nn.sigmoid(gsb_ref[...].astype(F32)) * br_sb
              + jax.nn.sigmoid(gpool_ref[...].astype(F32)) * br_pool)
    mix = jnp.dot(merged.astype(BF16), wout_ref[...], preferred_element_type=F32)
    x1 = x_ref[...] + gate1_ref[...] * mix
    x1_ref[...] = x1
    h2 = _rms_mod(x1, g2_ref[...], shift2_ref[...], scale2_ref[...])
    h2_ref[...] = h2.reshape(TM_PROJ, *ROW_TILE)
    lt_ref[...] = lax.dot_general(wr_ref[...], h2.astype(BF16), (((1,), (1,)), ((), ())),
                                  preferred_element_type=F32) + br_ref[...]


def _merge(x2, o_sb, proj, mods4, g2, poolw_bf, pool_scale, wsb_bf, wpool_bf, wout_bf, wr_bf, br):
    tiles_per_seq = SEQ // TM_PROJ
    halo_blocks = TM_PROJ // HALO
    mod_spec = lambda k: pl.BlockSpec((None, None, 1, D_MODEL),
                                      lambda i: (i // tiles_per_seq, k, 0, 0))
    full = lambda shape: pl.BlockSpec(shape, lambda i: (0,) * len(shape))
    return pl.pallas_call(
        _merge_kernel,
        grid=(N_TOK // TM_PROJ,),
        in_specs=[pl.BlockSpec((TM_PROJ, D_MODEL), lambda i: (i, 0)),
                  pl.BlockSpec((TM_PROJ, SB_WIDTH), lambda i: (i, 0)),
                  pl.BlockSpec((TM_PROJ, POOL_WIDTH), lambda i: (i, 3)),
                  pl.BlockSpec((HALO, POOL_WIDTH),
                               lambda i: (jnp.maximum(i * halo_blocks - 1, 0), 3)),
                  pl.BlockSpec((TM_PROJ, D_MODEL), lambda i: (i, 2)),
                  pl.BlockSpec((TM_PROJ, D_MODEL), lambda i: (i, 3)),
                  full((len(POOL_WINDOWS), POOL_GROUP_DIM, POOL_GROUP_DIM)),
                  full((1, POOL_WIDTH)),
                  full((SB_WIDTH, D_MODEL)),
                  full((POOL_WIDTH, D_MODEL)),
                  full((D_MODEL, D_MODEL)),
                  mod_spec(2), mod_spec(3), mod_spec(4),
                  full((1, D_MODEL)),
                  full((ROUTE_ROWS, D_MODEL)),
                  full((ROUTE_ROWS, 1))],
        out_specs=[pl.BlockSpec((TM_PROJ, D_MODEL), lambda i: (i, 0)),
                   pl.BlockSpec((TM_PROJ, *ROW_TILE), lambda i: (i, 0, 0)),
                   pl.BlockSpec((ROUTE_ROWS, TM_PROJ), lambda i: (0, i))],
        out_shape=[jax.ShapeDtypeStruct((N_TOK, D_MODEL), F32),
                   jax.ShapeDtypeStruct((N_TOK, *ROW_TILE), F32),
                   jax.ShapeDtypeStruct((ROUTE_ROWS, N_TOK), F32)],
        scratch_shapes=[pltpu.VMEM((TM_PROJ + HALO, POOL_WIDTH), F32)],
        compiler_params=_cparams(1),
        name="merge",
    )(x2, o_sb, proj, proj, proj, proj, poolw_bf, pool_scale, wsb_bf, wpool_bf, wout_bf,
      mods4, mods4, mods4, g2, wr_bf, br)


def _first_index_of(mask, idx, big):
    return jnp.min(jnp.where(mask, idx, big), axis=0, keepdims=True)


def _route_kernel(lt_ref, info_ref, slot_ref, cnt_ref, base_ref, info_scr):
    phase = pl.program_id(0)
    step = pl.program_id(1)
    tn = TN_ROUTE
    tile = pl.ds(pl.multiple_of(step * tn, tn), tn)
    row8 = lax.broadcasted_iota(jnp.int32, (8, tn), 0).astype(F32)
    erow = lax.broadcasted_iota(jnp.int32, (N_EXPERTS, tn), 0).astype(F32)

    @pl.when((phase == 0) & (step == 0))
    def _():
        base_ref[...] = jnp.zeros_like(base_ref)

    @pl.when(phase == 0)
    def _():
        gl = lt_ref[0:8, :]
        gmax = jnp.max(gl, axis=0, keepdims=True)
        gsel = _first_index_of(gl == gmax, row8, 8.0)
        p_group = 1.0 / jnp.sum(jnp.exp(gl - gmax), axis=0, keepdims=True)
        ig = lt_ref[8:16, :]
        for g in range(1, N_EXPERT_GROUPS):
            ig = jnp.where(gsel == float(g), lt_ref[8 + 8 * g:16 + 8 * g, :], ig)
        m1 = jnp.max(ig, axis=0, keepdims=True)
        i1 = _first_index_of(ig == m1, row8, 8.0)
        rest = jnp.where(row8 == i1, -jnp.inf, ig)
        m2 = jnp.max(rest, axis=0, keepdims=True)
        i2 = _first_index_of(rest == m2, row8, 8.0)
        e2 = jnp.exp(m2 - m1)
        w_first = p_group / (1.0 + e2)
        w_second = p_group * e2 / (1.0 + e2)
        e_first = gsel * EXPERTS_PER_GROUP + i1
        e_second = gsel * EXPERTS_PER_GROUP + i2
        hit_first = erow == e_first
        hit_second = erow == e_second
        onehot = jnp.where(hit_first | hit_second, 1.0, 0.0)
        r_i = lax.broadcasted_iota(jnp.int32, (tn, tn), 0)
        c_i = lax.broadcasted_iota(jnp.int32, (tn, tn), 1)
        tri = jnp.where(r_i < c_i, 1.0, 0.0).astype(BF16)
        rank = jnp.dot(onehot.astype(BF16), tri, preferred_element_type=F32) + base_ref[...]
        rank_first = jnp.sum(jnp.where(hit_first, rank, 0.0), axis=0, keepdims=True)
        rank_second = jnp.sum(jnp.where(hit_second, rank, 0.0), axis=0, keepdims=True)
        base_ref[...] = base_ref[...] + jnp.sum(onehot, axis=1, keepdims=True)
        info = jnp.zeros((8, tn), F32)
        for r, val in enumerate((e_first, e_second, rank_first, rank_second, w_first, w_second)):
            info = jnp.where(row8 == float(r), val, info)
        info_scr[:, tile] = info

    @pl.when(phase == 1)
    def _():
        counts = base_ref[...]
        padded = jnp.floor((counts + (TB_EXP - 1.0)) * (1.0 / TB_EXP)) * TB_EXP
        r_e = lax.broadcasted_iota(jnp.int32, (N_EXPERTS, N_EXPERTS), 0)
        c_e = lax.broadcasted_iota(jnp.int32, (N_EXPERTS, N_EXPERTS), 1)
        padded_row = jnp.sum(jnp.where(r_e == c_e, padded, 0.0), axis=0, keepdims=True)
        pad_start = jnp.sum(jnp.where(c_e < r_e, padded_row, 0.0), axis=1, keepdims=True)
        info = info_scr[:, tile]
        slot_first = jnp.sum(jnp.where(erow == info[0:1, :], pad_start, 0.0), axis=0, keepdims=True) + info[2:3, :]
        slot_second = jnp.sum(jnp.where(erow == info[1:2, :], pad_start, 0.0), axis=0, keepdims=True) + info[3:4, :]
        slots = jnp.where(row8 == 0.0, slot_first, jnp.where(row8 == 1.0, slot_second, 0.0))
        slot_ref[...] = slots.astype(jnp.int32)
        info_ref[...] = info
        cnt_ref[...] = counts


def _route(logits_t):
    n_steps = N_TOK // TN_ROUTE
    return pl.pallas_call(
        _route_kernel,
        grid=(2, n_steps),
        in_specs=[pl.BlockSpec((ROUTE_ROWS, TN_ROUTE), lambda p, i: (0, jnp.where(p == 0, i, n_steps - 1)))],
        out_specs=[pl.BlockSpec((8, TN_ROUTE), lambda p, i: (0, i * p)),
                   pl.BlockSpec((8, TN_ROUTE), lambda p, i: (0, i * p)),
                   pl.BlockSpec((N_EXPERTS, 1), lambda p, i: (0, 0))],
        out_shape=[jax.ShapeDtypeStruct((8, N_TOK), F32),
                   jax.ShapeDtypeStruct((8, N_TOK), jnp.int32),
                   jax.ShapeDtypeStruct((N_EXPERTS, 1), F32)],
        scratch_shapes=[pltpu.VMEM((N_EXPERTS, 1), F32),
                        pltpu.VMEM((8, N_TOK), F32)],
        compiler_params=_cparams(2),
        name="route",
    )(logits_t)


def _dispatch_kernel(pend_ref, s0_ref, s1_ref, h2_ref, xs_hbm, zero_ref, sem, zero_sem):
    @pl.when(pl.program_id(0) == 0)
    def _():
        zero_ref[...] = jnp.zeros_like(zero_ref)
        clears = []
        for e in range(N_EXPERTS):
            end = pend_ref[e]
            has_block = end > (pend_ref[e - 1] if e > 0 else 0)
            start = pl.multiple_of(jnp.maximum(end - TB_EXP, 0), TB_EXP)
            clears.append((has_block, pltpu.make_async_copy(zero_ref, xs_hbm.at[pl.ds(start, TB_EXP)], zero_sem)))
        for t in range(N_BLOCKS - (2 * N_TOK) // TB_EXP):
            start = pl.multiple_of(jnp.minimum(pend_ref[N_EXPERTS - 1] + t * TB_EXP, N_SLOTS - TB_EXP), TB_EXP)
            unused = pend_ref[N_EXPERTS - 1] + t * TB_EXP < N_SLOTS
            clears.append((unused, pltpu.make_async_copy(zero_ref, xs_hbm.at[pl.ds(start, TB_EXP)], zero_sem)))
        for has_block, copy in clears:
            pl.when(has_block)(copy.start)
        for has_block, copy in clears:
            pl.when(has_block)(copy.wait)

    def row_copy(j, slot):
        return pltpu.make_async_copy(h2_ref.at[j], xs_hbm.at[slot], sem)

    def issue(j, c):
        row_copy(j, s0_ref[j]).start(priority=0)
        row_copy(j, s1_ref[j]).start(priority=1)
        return c

    lax.fori_loop(0, TT_DISP, issue, 0, unroll=8)

    def drain(j, c):
        row_copy(j, 0).wait()
        row_copy(j, 0).wait()
        return c

    lax.fori_loop(0, TT_DISP, drain, 0, unroll=8)


def _dispatch(pad_end, slot0, slot1, h2):
    smem = lambda: pl.BlockSpec((TT_DISP,), lambda i, pe: (i,), memory_space=pltpu.SMEM)
    return pl.pallas_call(
        _dispatch_kernel,
        grid_spec=pltpu.PrefetchScalarGridSpec(
            num_scalar_prefetch=1,
            grid=(N_TOK // TT_DISP,),
            in_specs=[smem(), smem(),
                      pl.BlockSpec((TT_DISP, *ROW_TILE), lambda i, pe: (i, 0, 0))],
            out_specs=pl.BlockSpec(memory_space=pl.ANY),
            scratch_shapes=[pltpu.VMEM((TB_EXP, *ROW_TILE), F32),
                            pltpu.SemaphoreType.DMA(()),
                            pltpu.SemaphoreType.DMA(())]),
        out_shape=jax.ShapeDtypeStruct((N_SLOTS, *ROW_TILE), F32),
        compiler_params=_cparams(1),
        name="dispatch",
    )(pad_end, slot0, slot1, h2)


def _experts_kernel(be_ref, nused_ref, xs_ref, wg_ref, wu_ref, wd_ref, ys_ref):
    used = pl.program_id(0) < nused_ref[0]

    @pl.when(used)
    def _():
        x = xs_ref[...].reshape(TB_EXP, D_MODEL).astype(BF16)
        g = jnp.dot(x, wg_ref[...].astype(BF16), preferred_element_type=F32)
        u = jnp.dot(x, wu_ref[...].astype(BF16), preferred_element_type=F32)
        mid = (g * jax.nn.sigmoid(g) * u).astype(BF16)
        y = jnp.dot(mid, wd_ref[...].astype(BF16), preferred_element_type=F32)
        ys_ref[...] = y.reshape(TB_EXP, *ROW_TILE)

    @pl.when(jnp.logical_not(used))
    def _():
        ys_ref[...] = jnp.zeros_like(ys_ref)


def _experts(block_e, n_used, xs, wg_bf, wu_bf, wd_bf):
    blk = lambda i, be, nu: (jnp.minimum(i, nu[0] - 1), 0, 0)
    wsel = lambda i, be, nu: (be[i], 0, 0)
    return pl.pallas_call(
        _experts_kernel,
        grid_spec=pltpu.PrefetchScalarGridSpec(
            num_scalar_prefetch=2,
            grid=(N_BLOCKS,),
            in_specs=[pl.BlockSpec((TB_EXP, *ROW_TILE), blk),
                      pl.BlockSpec((None, D_MODEL, D_EXPERT), wsel),
                      pl.BlockSpec((None, D_MODEL, D_EXPERT), wsel),
                      pl.BlockSpec((None, D_EXPERT, D_MODEL), wsel)],
            out_specs=pl.BlockSpec((TB_EXP, *ROW_TILE), lambda i, be, nu: (i, 0, 0))),
        out_shape=jax.ShapeDtypeStruct((N_SLOTS, *ROW_TILE), F32),
        compiler_params=_cparams(1),
        name="experts",
    )(block_e, n_used, xs, wg_bf, wu_bf, wd_bf)


def _combine_kernel(s0_ref, s1_ref, s0_next_ref, s1_next_ref, ys_hbm, x1_ref, wts_ref, gate2_ref, gf_ref,
                    o_ref, y_ref, sem):
    i = pl.program_id(0)
    cur = i % 2

    def row_copy(slot, buf, pick, j):
        return pltpu.make_async_copy(ys_hbm.at[slot], y_ref.at[buf, pick, j], sem.at[buf])

    def issue(first_ref, second_ref, buf):
        def body(j, c):
            row_copy(first_ref[j], buf, 0, j).start(priority=0)
            row_copy(second_ref[j], buf, 1, j).start(priority=1)
            return c
        lax.fori_loop(0, TC_COMB, body, 0, unroll=8)

    @pl.when(i == 0)
    def _():
        issue(s0_ref, s1_ref, 0)

    @pl.when(i + 1 < pl.num_programs(0))
    def _():
        issue(s0_next_ref, s1_next_ref, 1 - cur)

    def drain(j, c):
        row_copy(0, cur, 0, j).wait()
        row_copy(0, cur, 1, j).wait()
        return c

    lax.fori_loop(0, TC_COMB, drain, 0, unroll=8)
    wts = wts_ref[...]
    y_first = y_ref[cur, 0].reshape(TC_COMB, D_MODEL)
    y_second = y_ref[cur, 1].reshape(TC_COMB, D_MODEL)
    moe = wts[:, 4:5] * y_first + wts[:, 5:6] * y_second
    x2 = x1_ref[...] + gate2_ref[...] * moe
    ms = jnp.mean(x2 * x2, axis=-1, keepdims=True)
    o_ref[...] = x2 * lax.rsqrt(ms + EPS) * gf_ref[...]


def _combine(slot0, slot1, ys, x1, wts_rows, mods4, gf):
    n_steps = N_TOK // TC_COMB
    tiles_per_seq = SEQ // TC_COMB
    smem = lambda: pl.BlockSpec((TC_COMB,), lambda i: (i,), memory_space=pltpu.SMEM)
    smem_next = lambda: pl.BlockSpec((TC_COMB,), lambda i: (jnp.minimum(i + 1, n_steps - 1),),
                                     memory_space=pltpu.SMEM)
    return pl.pallas_call(
        _combine_kernel,
        grid=(n_steps,),
        in_specs=[smem(), smem(), smem_next(), smem_next(),
                  pl.BlockSpec(memory_space=pl.ANY),
                  pl.BlockSpec((TC_COMB, D_MODEL), lambda i: (i, 0)),
                  pl.BlockSpec((TC_COMB, 8), lambda i: (i, 0)),
                  pl.BlockSpec((None, None, 1, D_MODEL), lambda i: (i // tiles_per_seq, 5, 0, 0)),
                  pl.BlockSpec((1, D_MODEL), lambda i: (0, 0))],
        out_specs=pl.BlockSpec((TC_COMB, D_MODEL), lambda i: (i, 0)),
        out_shape=jax.ShapeDtypeStruct((N_TOK, D_MODEL), F32),
        scratch_shapes=[pltpu.VMEM((2, 2, TC_COMB, *ROW_TILE), F32),
                        pltpu.SemaphoreType.DMA((2,))],
        compiler_params=_cparams(1),
        name="combine",
    )(slot0, slot1, slot0, slot1, ys, x1, wts_rows, mods4, gf)


def _layer(x2, mods4, norm1_g, norm2_g, w_in, pool_w, pool_scale, w_branch_sb, w_branch_pool,
           w_out, w_rg, b_rg, w_re, b_re, w_gate, w_up, w_down, norm_f_g):
    q_scale = jnp.where(jnp.arange(MIX_PROJ_WIDTH) < SB_WIDTH, HEAD_DIM ** -0.5, 1.0).astype(F32)
    w_in_bf = (w_in * q_scale[None, :]).astype(BF16)
    proj = _inproj(x2, norm1_g.reshape(1, D_MODEL), mods4, w_in_bf)
    o_sb = _attention(proj)

    wr = jnp.concatenate([w_rg.T, jnp.zeros((4, D_MODEL), F32),
                          w_re.transpose(0, 2, 1).reshape(N_EXPERTS, D_MODEL)], axis=0)
    br = jnp.concatenate([b_rg, jnp.full((4,), NEG_BIG, F32), b_re.reshape(N_EXPERTS)]).reshape(ROUTE_ROWS, 1)
    x1, h2, logits_t = _merge(
        x2, o_sb, proj, mods4, norm2_g.reshape(1, D_MODEL), pool_w.astype(BF16),
        pool_scale.reshape(1, POOL_WIDTH), w_branch_sb.astype(BF16), w_branch_pool.astype(BF16),
        w_out.astype(BF16), wr.astype(BF16), br)

    info, slots, counts = _route(logits_t)
    slot0, slot1 = slots[0], slots[1]
    counts = counts.reshape(N_EXPERTS).astype(jnp.int32)
    pad_end = jnp.cumsum(((counts + TB_EXP - 1) // TB_EXP) * TB_EXP)
    block_first = jnp.arange(N_BLOCKS, dtype=jnp.int32) * TB_EXP
    block_e = jnp.minimum(jnp.sum((pad_end[None, :] <= block_first[:, None]).astype(jnp.int32), axis=1),
                          N_EXPERTS - 1).astype(jnp.int32)
    n_used = (pad_end[-1:] // TB_EXP).astype(jnp.int32)

    xs = _dispatch(pad_end.astype(jnp.int32), slot0, slot1, h2)
    ys = _experts(block_e, n_used, xs, w_gate, w_up, w_down)
    return _combine(slot0, slot1, ys, x1, info.T, mods4, norm_f_g.reshape(1, D_MODEL))


def kernel(x, c, norm1_g, norm2_g, w_ada, b_ada, w_in, pool_w, pool_scale, w_branch_sb, w_branch_pool, w_out, w_route_group, b_route_group, w_route_expert, b_route_expert, w_exp_gate, w_exp_up, w_exp_down, norm_f_g):
    assert x.shape == (BATCH, SEQ, D_MODEL) and w_ada.shape[0] == 1
    mods = _ada(c, w_ada[0], b_ada[0])
    mods4 = mods.reshape(BATCH, 6, 1, D_MODEL)
    out = _layer(x.reshape(N_TOK, D_MODEL), mods4, norm1_g[0], norm2_g[0], w_in[0], pool_w[0],
                 pool_scale[0], w_branch_sb[0], w_branch_pool[0], w_out[0], w_route_group[0],
                 b_route_group[0], w_route_expert[0], b_route_expert[0], w_exp_gate[0], w_exp_up[0],
                 w_exp_down[0], norm_f_g)
    return out.reshape(BATCH, SEQ, D_MODEL)
```

```python
import functools

import jax
import jax.numpy as jnp
from jax import lax
from jax.experimental import pallas as pl
from jax.experimental.pallas import tpu as pltpu

F32 = jnp.float32
BF16 = jnp.bfloat16

D_MODEL = 1024
BATCH = 16
SEQ = 2048
N_TOK = BATCH * SEQ
HEAD_DIM = 64
SB_WIDTH = 512
POOL_WIDTH = 512
POOL_WINDOWS = (2, 4, 8, 16)
POOL_GROUP_DIM = 128
MIX_PROJ_WIDTH = 4096
N_EXPERT_GROUPS = 4
EXPERTS_PER_GROUP = 8
N_EXPERTS = 32
D_EXPERT = 512
EPS = 1e-6

LANES = 128
SUBLANES = 8
ROW_TILE = (SUBLANES, D_MODEL // SUBLANES)
assert ROW_TILE[1] == LANES
HALO = 16
TM_PROJ = 512
TQ = 256
TK = 256
MXU_DIM = 256
ATTN_HEADS = 8
ATTN_WIDTH = ATTN_HEADS * HEAD_DIM
LOG2_E = 1.4426950408889634
UNDERFLOW_LOG2 = 160.0
TN_ROUTE = 512
TN_SLOT = 4096
TT_DISP = 1024
TB_EXP = 512
TC_COMB = 256
N_BLOCKS = (2 * N_TOK) // TB_EXP + N_EXPERTS
N_SLOTS = N_BLOCKS * TB_EXP
ROUTE_ROWS = 40
NEG_BIG = -1e30
VMEM_LIMIT = 56 * 1024 * 1024


def _cparams(n_axes):
    return pltpu.CompilerParams(dimension_semantics=("arbitrary",) * n_axes,
                                vmem_limit_bytes=VMEM_LIMIT)


def _ada_kernel(c_ref, w_ref, b_ref, o_ref):
    c = c_ref[...]
    ca = (c * jax.nn.sigmoid(c)).astype(BF16)
    o_ref[...] = jnp.dot(ca, w_ref[...].astype(BF16), preferred_element_type=F32) + b_ref[...]


def _ada(c, w_ada, b_ada):
    n_out = w_ada.shape[1]
    tn = 1024
    return pl.pallas_call(
        _ada_kernel,
        grid=(n_out // tn,),
        in_specs=[pl.BlockSpec((BATCH, D_MODEL), lambda j: (0, 0)),
                  pl.BlockSpec((D_MODEL, tn), lambda j: (0, j)),
                  pl.BlockSpec((1, tn), lambda j: (0, j))],
        out_specs=pl.BlockSpec((BATCH, tn), lambda j: (0, j)),
        out_shape=jax.ShapeDtypeStruct((BATCH, n_out), F32),
        compiler_params=_cparams(1),
        name="ada",
    )(c, w_ada, b_ada.reshape(1, n_out))


def _rms_mod(x, g, shift, scale):
    ms = jnp.mean(x * x, axis=-1, keepdims=True)
    y = x * lax.rsqrt(ms + EPS) * g
    return y * (1.0 + scale) + shift


def _inproj_kernel(x_ref, g_ref, sh_ref, sc_ref, w_ref, o_ref):
    h = _rms_mod(x_ref[...], g_ref[...], sh_ref[...], sc_ref[...]).astype(BF16)
    o_ref[:, 0:SB_WIDTH] = (jnp.dot(h, w_ref[:, 0:SB_WIDTH], preferred_element_type=F32) * LOG2_E).astype(BF16)
    for lo in range(SB_WIDTH, MIX_PROJ_WIDTH, SB_WIDTH):
        o_ref[:, lo:lo + SB_WIDTH] = jnp.dot(
            h, w_ref[:, lo:lo + SB_WIDTH], preferred_element_type=F32).astype(BF16)


def _inproj(x2, g1, mods4, w_in_bf):
    tiles_per_seq = SEQ // TM_PROJ
    mod_spec = lambda k: pl.BlockSpec((None, None, 1, D_MODEL),
                                      lambda i: (i // tiles_per_seq, k, 0, 0))
    return pl.pallas_call(
        _inproj_kernel,
        grid=(N_TOK // TM_PROJ,),
        in_specs=[pl.BlockSpec((TM_PROJ, D_MODEL), lambda i: (i, 0)),
                  pl.BlockSpec((1, D_MODEL), lambda i: (0, 0)),
                  mod_spec(0), mod_spec(1),
                  pl.BlockSpec((D_MODEL, MIX_PROJ_WIDTH), lambda i: (0, 0))],
        out_specs=pl.BlockSpec((TM_PROJ, MIX_PROJ_WIDTH), lambda i: (i, 0)),
        out_shape=jax.ShapeDtypeStruct((N_TOK, MIX_PROJ_WIDTH), BF16),
        compiler_params=_cparams(1),
        name="inproj",
    )(x2, g1, mods4, mods4, w_in_bf)


def _attn_kernel(q_ref, k_ref, v_ref, o_ref, z_a, z_b, a_a, a_b, acc_ref, r_ref):
    qi = pl.program_id(2)
    head_q = lax.broadcasted_iota(jnp.int32, (TQ, MXU_DIM), 1) // HEAD_DIM
    head_k = lax.broadcasted_iota(jnp.int32, (TK, MXU_DIM), 1) // HEAD_DIM
    row = lax.broadcasted_iota(jnp.int32, (TQ, TK), 0)
    col = lax.broadcasted_iota(jnp.int32, (TQ, TK), 1)
    past = col < row
    upper = jnp.where(row > col, 1.0, 0.0).astype(BF16)
    heads_per_group = MXU_DIM // HEAD_DIM
    n_groups = ATTN_WIDTH // MXU_DIM

    def group_cols(g):
        return slice(g * MXU_DIM, (g + 1) * MXU_DIM)

    def scores(kb, z_out):
        start = pl.multiple_of(kb * TK, TK)
        for g in range(n_groups):
            qg = q_ref[:, group_cols(g)]
            k = k_ref[pl.ds(start, TK), group_cols(g)]
            for hh in range(heads_per_group):
                q_h = jnp.where(head_q == hh, qg, jnp.zeros_like(qg))
                z_out[g * heads_per_group + hh] = lax.dot_general(
                    q_h, k, (((1,), (1,)), ((), ())), preferred_element_type=F32)

    def weights(z_in, a_out, diag):
        for h in range(ATTN_HEADS):
            z2 = z_in[h]
            sp2 = jnp.maximum(z2, 0.0) + jnp.log(1.0 + jnp.exp2(-jnp.abs(z2))) * LOG2_E
            stay = jnp.where(past, sp2, 0.0) if diag else sp2
            cum = jnp.dot(stay.astype(BF16), upper, preferred_element_type=F32)
            a = jnp.exp2(z2 - (sp2 + cum + r_ref[h]))
            if diag:
                a = jnp.where(past, a, 0.0)
            a_out[h] = a.astype(BF16)
            r_ref[h] = r_ref[h] + jnp.sum(stay, axis=-1, keepdims=True)

    def accumulate(kb, a_in):
        start = pl.multiple_of(kb * TK, TK)
        for g in range(n_groups):
            v = v_ref[pl.ds(start, TK), group_cols(g)]
            acc = acc_ref[g]
            for hh in range(heads_per_group):
                v_h = jnp.where(head_k == hh, v, jnp.zeros_like(v))
                acc = acc + jnp.dot(a_in[g * heads_per_group + hh], v_h, preferred_element_type=F32)
            acc_ref[g] = acc

    acc_ref[...] = jnp.zeros_like(acc_ref)
    r_ref[...] = jnp.zeros_like(r_ref)
    scores(qi, z_a)
    scores(jnp.maximum(qi - 1, 0), z_b)
    weights(z_a, a_a, True)

    def step(carry):
        i, _ = carry
        kb = qi - i
        nxt = jnp.maximum(kb - 1, 0)

        @pl.when(i % 2 == 1)
        def _():
            accumulate(kb + 1, a_a)
            scores(nxt, z_a)
            weights(z_b, a_b, False)

        @pl.when(i % 2 == 0)
        def _():
            accumulate(kb + 1, a_b)
            scores(nxt, z_b)
            weights(z_a, a_a, False)
        return i + 1, jnp.min(r_ref[...]) >= UNDERFLOW_LOG2

    n_done, _ = lax.while_loop(lambda c: jnp.logical_and(c[0] <= qi, jnp.logical_not(c[1])),
                               step, (jnp.int32(1), False))
    last = n_done - 1

    @pl.when(last % 2 == 1)
    def _():
        accumulate(qi - last, a_b)

    @pl.when(last % 2 == 0)
    def _():
        accumulate(qi - last, a_a)

    for g in range(n_groups):
        o_ref[:, group_cols(g)] = acc_ref[g].astype(BF16)


def _attention(proj):
    n_q = SEQ // TQ
    n_hg = SB_WIDTH // ATTN_WIDTH
    return pl.pallas_call(
        _attn_kernel,
        grid=(BATCH, n_hg, n_q),
        in_specs=[pl.BlockSpec((TQ, ATTN_WIDTH), lambda b, h, i: (b * n_q + i, h)),
                  pl.BlockSpec((SEQ, ATTN_WIDTH), lambda b, h, i: (b, n_hg + h)),
                  pl.BlockSpec((SEQ, ATTN_WIDTH), lambda b, h, i: (b, 2 * n_hg + h))],
        out_specs=pl.BlockSpec((TQ, ATTN_WIDTH), lambda b, h, i: (b * n_q + i, h)),
        out_shape=jax.ShapeDtypeStruct((N_TOK, SB_WIDTH), BF16),
        scratch_shapes=[pltpu.VMEM((ATTN_HEADS, TQ, TK), F32), pltpu.VMEM((ATTN_HEADS, TQ, TK), F32),
                        pltpu.VMEM((ATTN_HEADS, TQ, TK), BF16), pltpu.VMEM((ATTN_HEADS, TQ, TK), BF16),
                        pltpu.VMEM((ATTN_WIDTH // MXU_DIM, TQ, MXU_DIM), F32),
                        pltpu.VMEM((ATTN_HEADS, TQ, 1), F32)],
        compiler_params=_cparams(3),
        name="attn",
    )(proj, proj, proj)


def _merge_kernel(x_ref, osb_ref, p_ref, halo_ref, gsb_ref, gpool_ref,
                  poolw_ref, pscale_ref, wsb_ref, wpool_ref, wout_ref,
                  gate1_ref, shift2_ref, scale2_ref, g2_ref, wr_ref, br_ref,
                  x1_ref, h2_ref, lt_ref, pext_ref):
    i = pl.program_id(0)
    tiles_per_seq = SEQ // TM_PROJ
    t0 = (i % tiles_per_seq) * TM_PROJ
    halo = halo_ref[...].astype(F32)
    pext_ref[0:HALO, :] = jnp.where(t0 == 0, jnp.zeros_like(halo), halo)
    pext_ref[HALO:, :] = p_ref[...].astype(F32)
    pos = t0 + lax.broadcasted_iota(jnp.int32, (TM_PROJ, 1), 0)
    o_pool = []
    for g, w in enumerate(POOL_WINDOWS):
        cols = slice(g * POOL_GROUP_DIM, (g + 1) * POOL_GROUP_DIM)
        cur = pext_ref[HALO:, cols]
        win = cur
        for d in range(1, w):
            win = win + pext_ref[HALO - d:HALO - d + TM_PROJ, cols]
        count = jnp.minimum(pos + 1, w).astype(F32)
        pooled = win / count - cur
        og = jnp.dot(pooled.astype(BF16), poolw_ref[g], preferred_element_type=F32)
        o_pool.append(og * pscale_ref[:, cols])
    o_pool = jnp.concatenate(o_pool, axis=-1).astype(BF16)
    br_sb = jnp.dot(osb_ref[...], wsb_ref[...], preferred_element_type=F32)
    br_pool = jnp.dot(o_pool, wpool_ref[...], preferred_element_type=F32)
    merged = (jax.nn.sigmoid(gsb_ref[...].astype(F32)) * br_sb
              + jax.nn.sigmoid(gpool_ref[...].astype(F32)) * br_pool)
    mix = jnp.dot(merged.astype(BF16), wout_ref[...], preferred_element_type=F32)
    x1 = x_ref[...] + gate1_ref[...] * mix
    x1_ref[...] = x1
    h2 = _rms_mod(x1, g2_ref[...], shift2_ref[...], scale2_ref[...])
    h2_ref[...] = h2.reshape(TM_PROJ, *ROW_TILE)
    lt_ref[...] = lax.dot_general(wr_ref[...], h2.astype(BF16), (((1,), (1,)), ((), ())),
                                  preferred_element_type=F32) + br_ref[...]


def _merge(x2, o_sb, proj, mods4, g2, poolw_bf, pool_scale, wsb_bf, wpool_bf, wout_bf, wr_bf, br):
    tiles_per_seq = SEQ // TM_PROJ
    halo_blocks = TM_PROJ // HALO
    mod_spec = lambda k: pl.BlockSpec((None, None, 1, D_MODEL),
                                      lambda i: (i // tiles_per_seq, k, 0, 0))
    full = lambda shape: pl.BlockSpec(shape, lambda i: (0,) * len(shape))
    return pl.pallas_call(
        _merge_kernel,
        grid=(N_TOK // TM_PROJ,),
        in_specs=[pl.BlockSpec((TM_PROJ, D_MODEL), lambda i: (i, 0)),
                  pl.BlockSpec((TM_PROJ, SB_WIDTH), lambda i: (i, 0)),
                  pl.BlockSpec((TM_PROJ, POOL_WIDTH), lambda i: (i, 3)),
                  pl.BlockSpec((HALO, POOL_WIDTH),
                               lambda i: (jnp.maximum(i * halo_blocks - 1, 0), 3)),
                  pl.BlockSpec((TM_PROJ, D_MODEL), lambda i: (i, 2)),
                  pl.BlockSpec((TM_PROJ, D_MODEL), lambda i: (i, 3)),
                  full((len(POOL_WINDOWS), POOL_GROUP_DIM, POOL_GROUP_DIM)),
                  full((1, POOL_WIDTH)),
                  full((SB_WIDTH, D_MODEL)),
                  full((POOL_WIDTH, D_MODEL)),
                  full((D_MODEL, D_MODEL)),
                  mod_spec(2), mod_spec(3), mod_spec(4),
                  full((1, D_MODEL)),
                  full((ROUTE_ROWS, D_MODEL)),
                  full((ROUTE_ROWS, 1))],
        out_specs=[pl.BlockSpec((TM_PROJ, D_MODEL), lambda i: (i, 0)),
                   pl.BlockSpec((TM_PROJ, *ROW_TILE), lambda i: (i, 0, 0)),
                   pl.BlockSpec((ROUTE_ROWS, TM_PROJ), lambda i: (0, i))],
        out_shape=[jax.ShapeDtypeStruct((N_TOK, D_MODEL), F32),
                   jax.ShapeDtypeStruct((N_TOK, *ROW_TILE), F32),
                   jax.ShapeDtypeStruct((ROUTE_ROWS, N_TOK), F32)],
        scratch_shapes=[pltpu.VMEM((TM_PROJ + HALO, POOL_WIDTH), F32)],
        compiler_params=_cparams(1),
        name="merge",
    )(x2, o_sb, proj, proj, proj, proj, poolw_bf, pool_scale, wsb_bf, wpool_bf, wout_bf,
      mods4, mods4, mods4, g2, wr_bf, br)


def _first_index_of(mask, idx, big):
    return jnp.min(jnp.where(mask, idx, big), axis=0, keepdims=True)


def _route_kernel(lt_ref, info_ref, slot_ref, cnt_ref, base_ref, info_scr):
    step = pl.program_id(0)
    n_rank_steps = N_TOK // TN_ROUTE
    tn = TN_ROUTE

    @pl.when(step == 0)
    def _():
        base_ref[...] = jnp.zeros_like(base_ref)

    @pl.when(step < n_rank_steps)
    def _():
        tile = pl.ds(pl.multiple_of(step * tn, tn), tn)
        row8 = lax.broadcasted_iota(jnp.int32, (8, tn), 0).astype(F32)
        erow = lax.broadcasted_iota(jnp.int32, (N_EXPERTS, tn), 0).astype(F32)
        gl = lt_ref[0:8, :]
        gmax = jnp.max(gl, axis=0, keepdims=True)
        gsel = _first_index_of(gl == gmax, row8, 8.0)
        p_group = 1.0 / jnp.sum(jnp.exp(gl - gmax), axis=0, keepdims=True)
        ig = lt_ref[8:16, :]
        for g in range(1, N_EXPERT_GROUPS):
            ig = jnp.where(gsel == float(g), lt_ref[8 + 8 * g:16 + 8 * g, :], ig)
        m1 = jnp.max(ig, axis=0, keepdims=True)
        i1 = _first_index_of(ig == m1, row8, 8.0)
        rest = jnp.where(row8 == i1, -jnp.inf, ig)
        m2 = jnp.max(rest, axis=0, keepdims=True)
        i2 = _first_index_of(rest == m2, row8, 8.0)
        e2 = jnp.exp(m2 - m1)
        w_first = p_group / (1.0 + e2)
        w_second = p_group * e2 / (1.0 + e2)
        e_first = gsel * EXPERTS_PER_GROUP + i1
        e_second = gsel * EXPERTS_PER_GROUP + i2
        hit_first = erow == e_first
        hit_second = erow == e_second
        onehot = jnp.where(hit_first | hit_second, 1.0, 0.0)
        r_i = lax.broadcasted_iota(jnp.int32, (tn, tn), 0)
        c_i = lax.broadcasted_iota(jnp.int32, (tn, tn), 1)
        tri = jnp.where(r_i < c_i, 1.0, 0.0).astype(BF16)
        rank = jnp.dot(onehot.astype(BF16), tri, preferred_element_type=F32) + base_ref[...]
        rank_first = jnp.sum(jnp.where(hit_first, rank, 0.0), axis=0, keepdims=True)
        rank_second = jnp.sum(jnp.where(hit_second, rank, 0.0), axis=0, keepdims=True)
        base_ref[...] = base_ref[...] + jnp.sum(onehot, axis=1, keepdims=True)
        info = jnp.zeros((8, tn), F32)
        for r, val in enumerate((e_first, e_second, rank_first, rank_second, w_first, w_second)):
            info = jnp.where(row8 == float(r), val, info)
        info_scr[:, tile] = info

    @pl.when(step >= n_rank_steps)
    def _():
        ts = TN_SLOT
        tile = pl.ds(pl.multiple_of((step - n_rank_steps) * ts, ts), ts)
        row8 = lax.broadcasted_iota(jnp.int32, (8, ts), 0).astype(F32)
        erow = lax.broadcasted_iota(jnp.int32, (N_EXPERTS, ts), 0).astype(F32)
        counts = base_ref[...]
        padded = jnp.floor((counts + (TB_EXP - 1.0)) * (1.0 / TB_EXP)) * TB_EXP
        r_e = lax.broadcasted_iota(jnp.int32, (N_EXPERTS, N_EXPERTS), 0)
        c_e = lax.broadcasted_iota(jnp.int32, (N_EXPERTS, N_EXPERTS), 1)
        padded_row = jnp.sum(jnp.where(r_e == c_e, padded, 0.0), axis=0, keepdims=True)
        pad_start = jnp.sum(jnp.where(c_e < r_e, padded_row, 0.0), axis=1, keepdims=True)
        info = info_scr[:, tile]
        slot_first = jnp.sum(jnp.where(erow == info[0:1, :], pad_start, 0.0), axis=0, keepdims=True) + info[2:3, :]
        slot_second = jnp.sum(jnp.where(erow == info[1:2, :], pad_start, 0.0), axis=0, keepdims=True) + info[3:4, :]
        slots = jnp.where(row8 == 0.0, slot_first, jnp.where(row8 == 1.0, slot_second, 0.0))
        slot_ref[...] = slots.astype(jnp.int32)
        info_ref[...] = info
        cnt_ref[...] = counts


def _route(logits_t):
    n_rank_steps = N_TOK // TN_ROUTE
    n_slot_steps = N_TOK // TN_SLOT
    slot_blk = lambda s: (0, jnp.maximum(s - n_rank_steps, 0))
    return pl.pallas_call(
        _route_kernel,
        grid=(n_rank_steps + n_slot_steps,),
        in_specs=[pl.BlockSpec((ROUTE_ROWS, TN_ROUTE), lambda s: (0, jnp.minimum(s, n_rank_steps - 1)))],
        out_specs=[pl.BlockSpec((8, TN_SLOT), slot_blk),
                   pl.BlockSpec((8, TN_SLOT), slot_blk),
                   pl.BlockSpec((N_EXPERTS, 1), lambda s: (0, 0))],
        out_shape=[jax.ShapeDtypeStruct((8, N_TOK), F32),
                   jax.ShapeDtypeStruct((8, N_TOK), jnp.int32),
                   jax.ShapeDtypeStruct((N_EXPERTS, 1), F32)],
        scratch_shapes=[pltpu.VMEM((N_EXPERTS, 1), F32),
                        pltpu.VMEM((8, N_TOK), F32)],
        compiler_params=_cparams(1),
        name="route",
    )(logits_t)


def _dispatch_kernel(pend_ref, s0_ref, s1_ref, h2_ref, xs_hbm, zero_ref, sem, zero_sem):
    @pl.when(pl.program_id(0) == 0)
    def _():
        zero_ref[...] = jnp.zeros_like(zero_ref)
        clears = []
        for e in range(N_EXPERTS):
            end = pend_ref[e]
            has_block = end > (pend_ref[e - 1] if e > 0 else 0)
            start = pl.multiple_of(jnp.maximum(end - TB_EXP, 0), TB_EXP)
            clears.append((has_block, pltpu.make_async_copy(zero_ref, xs_hbm.at[pl.ds(start, TB_EXP)], zero_sem)))
        for t in range(N_BLOCKS - (2 * N_TOK) // TB_EXP):
            start = pl.multiple_of(jnp.minimum(pend_ref[N_EXPERTS - 1] + t * TB_EXP, N_SLOTS - TB_EXP), TB_EXP)
            unused = pend_ref[N_EXPERTS - 1] + t * TB_EXP < N_SLOTS
            clears.append((unused, pltpu.make_async_copy(zero_ref, xs_hbm.at[pl.ds(start, TB_EXP)], zero_sem)))
        for has_block, copy in clears:
            pl.when(has_block)(copy.start)
        for has_block, copy in clears:
            pl.when(has_block)(copy.wait)

    def row_copy(j, slot):
        return pltpu.make_async_copy(h2_ref.at[j], xs_hbm.at[slot], sem)

    def issue(j, c):
        row_copy(j, s0_ref[j]).start(priority=0)
        row_copy(j, s1_ref[j]).start(priority=1)
        return c

    lax.fori_loop(0, TT_DISP, issue, 0, unroll=8)

    def drain(j, c):
        row_copy(j, 0).wait()
        row_copy(j, 0).wait()
        return c

    lax.fori_loop(0, TT_DISP, drain, 0, unroll=8)


def _dispatch(pad_end, slot0, slot1, h2):
    smem = lambda: pl.BlockSpec((TT_DISP,), lambda i, pe: (i,), memory_space=pltpu.SMEM)
    return pl.pallas_call(
        _dispatch_kernel,
        grid_spec=pltpu.PrefetchScalarGridSpec(
            num_scalar_prefetch=1,
            grid=(N_TOK // TT_DISP,),
            in_specs=[smem(), smem(),
                      pl.BlockSpec((TT_DISP, *ROW_TILE), lambda i, pe: (i, 0, 0))],
            out_specs=pl.BlockSpec(memory_space=pl.ANY),
            scratch_shapes=[pltpu.VMEM((TB_EXP, *ROW_TILE), F32),
                            pltpu.SemaphoreType.DMA(()),
                            pltpu.SemaphoreType.DMA(())]),
        out_shape=jax.ShapeDtypeStruct((N_SLOTS, *ROW_TILE), F32),
        compiler_params=_cparams(1),
        name="dispatch",
    )(pad_end, slot0, slot1, h2)


def _experts_kernel(be_ref, nused_ref, xs_ref, wg_ref, wu_ref, wd_ref, ys_ref):
    used = pl.program_id(0) < nused_ref[0]

    @pl.when(used)
    def _():
        x = xs_ref[...].reshape(TB_EXP, D_MODEL).astype(BF16)
        g = jnp.dot(x, wg_ref[...].astype(BF16), preferred_element_type=F32)
        u = jnp.dot(x, wu_ref[...].astype(BF16), preferred_element_type=F32)
        mid = (g * jax.nn.sigmoid(g) * u).astype(BF16)
        y = jnp.dot(mid, wd_ref[...].astype(BF16), preferred_element_type=F32)
        ys_ref[...] = y.reshape(TB_EXP, *ROW_TILE)

    @pl.when(jnp.logical_not(used))
    def _():
        ys_ref[...] = jnp.zeros_like(ys_ref)


def _experts(block_e, n_used, xs, wg_bf, wu_bf, wd_bf):
    blk = lambda i, be, nu: (jnp.minimum(i, nu[0] - 1), 0, 0)
    wsel = lambda i, be, nu: (be[i], 0, 0)
    return pl.pallas_call(
        _experts_kernel,
        grid_spec=pltpu.PrefetchScalarGridSpec(
            num_scalar_prefetch=2,
            grid=(N_BLOCKS,),
            in_specs=[pl.BlockSpec((TB_EXP, *ROW_TILE), blk),
                      pl.BlockSpec((None, D_MODEL, D_EXPERT), wsel),
                      pl.BlockSpec((None, D_MODEL, D_EXPERT), wsel),
                      pl.BlockSpec((None, D_EXPERT, D_MODEL), wsel)],
            out_specs=pl.BlockSpec((TB_EXP, *ROW_TILE), lambda i, be, nu: (i, 0, 0))),
        out_shape=jax.ShapeDtypeStruct((N_SLOTS, *ROW_TILE), F32),
        compiler_params=_cparams(1),
        name="experts",
    )(block_e, n_used, xs, wg_bf, wu_bf, wd_bf)


def _combine_kernel(s0_ref, s1_ref, s0_next_ref, s1_next_ref, ys_hbm, x1_ref, wts_ref, gate2_ref, gf_ref,
                    o_ref, y_ref, sem):
    i = pl.program_id(0)
    cur = i % 2

    def row_copy(slot, buf, pick, j):
        return pltpu.make_async_copy(ys_hbm.at[slot], y_ref.at[buf, pick, j], sem.at[buf])

    def issue(first_ref, second_ref, buf):
        def body(j, c):
            row_copy(first_ref[j], buf, 0, j).start(priority=0)
            row_copy(second_ref[j], buf, 1, j).start(priority=1)
            return c
        lax.fori_loop(0, TC_COMB, body, 0, unroll=8)

    @pl.when(i == 0)
    def _():
        issue(s0_ref, s1_ref, 0)

    @pl.when(i + 1 < pl.num_programs(0))
    def _():
        issue(s0_next_ref, s1_next_ref, 1 - cur)

    def drain(j, c):
        row_copy(0, cur, 0, j).wait()
        row_copy(0, cur, 1, j).wait()
        return c

    lax.fori_loop(0, TC_COMB, drain, 0, unroll=8)
    wts = wts_ref[...]
    y_first = y_ref[cur, 0].reshape(TC_COMB, D_MODEL)
    y_second = y_ref[cur, 1].reshape(TC_COMB, D_MODEL)
    moe = wts[:, 4:5] * y_first + wts[:, 5:6] * y_second
    x2 = x1_ref[...] + gate2_ref[...] * moe
    ms = jnp.mean(x2 * x2, axis=-1, keepdims=True)
    o_ref[...] = x2 * lax.rsqrt(ms + EPS) * gf_ref[...]


def _combine(slot0, slot1, ys, x1, wts_rows, mods4, gf):
    n_steps = N_TOK // TC_COMB
    tiles_per_seq = SEQ // TC_COMB
    smem = lambda: pl.BlockSpec((TC_COMB,), lambda i: (i,), memory_space=pltpu.SMEM)
    smem_next = lambda: pl.BlockSpec((TC_COMB,), lambda i: (jnp.minimum(i + 1, n_steps - 1),),
                                     memory_space=pltpu.SMEM)
    return pl.pallas_call(
        _combine_kernel,
        grid=(n_steps,),
        in_specs=[smem(), smem(), smem_next(), smem_next(),
                  pl.BlockSpec(memory_space=pl.ANY),
                  pl.BlockSpec((TC_COMB, D_MODEL), lambda i: (i, 0)),
                  pl.BlockSpec((TC_COMB, 8), lambda i: (i, 0)),
                  pl.BlockSpec((None, None, 1, D_MODEL), lambda i: (i // tiles_per_seq, 5, 0, 0)),
                  pl.BlockSpec((1, D_MODEL), lambda i: (0, 0))],
        out_specs=pl.BlockSpec((TC_COMB, D_MODEL), lambda i: (i, 0)),
        out_shape=jax.ShapeDtypeStruct((N_TOK, D_MODEL), F32),
        scratch_shapes=[pltpu.VMEM((2, 2, TC_COMB, *ROW_TILE), F32),
                        pltpu.SemaphoreType.DMA((2,))],
        compiler_params=_cparams(1),
        name="combine",
    )(slot0, slot1, slot0, slot1, ys, x1, wts_rows, mods4, gf)


def _layer(x2, mods4, norm1_g, norm2_g, w_in, pool_w, pool_scale, w_branch_sb, w_branch_pool,
           w_out, w_rg, b_rg, w_re, b_re, w_gate, w_up, w_down, norm_f_g):
    q_scale = jnp.where(jnp.arange(MIX_PROJ_WIDTH) < SB_WIDTH, HEAD_DIM ** -0.5, 1.0).astype(F32)
    w_in_bf = (w_in * q_scale[None, :]).astype(BF16)
    proj = _inproj(x2, norm1_g.reshape(1, D_MODEL), mods4, w_in_bf)
    o_sb = _attention(proj)

    wr = jnp.concatenate([w_rg.T, jnp.zeros((4, D_MODEL), F32),
                          w_re.transpose(0, 2, 1).reshape(N_EXPERTS, D_MODEL)], axis=0)
    br = jnp.concatenate([b_rg, jnp.full((4,), NEG_BIG, F32), b_re.reshape(N_EXPERTS)]).reshape(ROUTE_ROWS, 1)
    x1, h2, logits_t = _merge(
        x2, o_sb, proj, mods4, norm2_g.reshape(1, D_MODEL), pool_w.astype(BF16),
        pool_scale.reshape(1, POOL_WIDTH), w_branch_sb.astype(BF16), w_branch_pool.astype(BF16),
        w_out.astype(BF16), wr.astype(BF16), br)

    info, slots, counts = _route(logits_t)
    slot0, slot1 = slots[0], slots[1]
    counts = counts.reshape(N_EXPERTS).astype(jnp.int32)
    pad_end = jnp.cumsum(((counts + TB_EXP - 1) // TB_EXP) * TB_EXP)
    block_first = jnp.arange(N_BLOCKS, dtype=jnp.int32) * TB_EXP
    block_e = jnp.minimum(jnp.sum((pad_end[None, :] <= block_first[:, None]).astype(jnp.int32), axis=1),
                          N_EXPERTS - 1).astype(jnp.int32)
    n_used = (pad_end[-1:] // TB_EXP).astype(jnp.int32)

    xs = _dispatch(pad_end.astype(jnp.int32), slot0, slot1, h2)
    ys = _experts(block_e, n_used, xs, w_gate, w_up, w_down)
    return _combine(slot0, slot1, ys, x1, info.T, mods4, norm_f_g.reshape(1, D_MODEL))


def kernel(x, c, norm1_g, norm2_g, w_ada, b_ada, w_in, pool_w, pool_scale, w_branch_sb, w_branch_pool, w_out, w_route_group, b_route_group, w_route_expert, b_route_expert, w_exp_gate, w_exp_up, w_exp_down, norm_f_g):
    assert x.shape == (BATCH, SEQ, D_MODEL) and w_ada.shape[0] == 1
    mods = _ada(c, w_ada[0], b_ada[0])
    mods4 = mods.reshape(BATCH, 6, 1, D_MODEL)
    out = _layer(x.reshape(N_TOK, D_MODEL), mods4, norm1_g[0], norm2_g[0], w_in[0], pool_w[0],
                 pool_scale[0], w_branch_sb[0], w_branch_pool[0], w_out[0], w_route_group[0],
                 b_route_group[0], w_route_expert[0], b_route_expert[0], w_exp_gate[0], w_exp_up[0],
                 w_exp_down[0], norm_f_g)
    return out.reshape(BATCH, SEQ, D_MODEL)
```

```python
import functools

import jax
import jax.numpy as jnp
from jax import lax
from jax.experimental import pallas as pl
from jax.experimental.pallas import tpu as pltpu

F32 = jnp.float32
BF16 = jnp.bfloat16

D_MODEL = 1024
BATCH = 16
SEQ = 2048
N_TOK = BATCH * SEQ
HEAD_DIM = 64
SB_WIDTH = 512
POOL_WIDTH = 512
POOL_WINDOWS = (2, 4, 8, 16)
POOL_GROUP_DIM = 128
MIX_PROJ_WIDTH = 4096
N_EXPERT_GROUPS = 4
EXPERTS_PER_GROUP = 8
N_EXPERTS = 32
D_EXPERT = 512
EPS = 1e-6

LANES = 128
SUBLANES = 8
ROW_TILE = (SUBLANES, D_MODEL // SUBLANES)
assert ROW_TILE[1] == LANES
HALO = 16
TM_PROJ = 512
TQ = 256
TK = 256
MXU_DIM = 256
ATTN_HEADS = 8
ATTN_WIDTH = ATTN_HEADS * HEAD_DIM
LOG2_E = 1.4426950408889634
UNDERFLOW_LOG2 = 160.0
TN_ROUTE = 512
TN_SLOT = 4096
TT_DISP = 1024
TB_EXP = 512
TC_COMB = 256
N_BLOCKS = (2 * N_TOK) // TB_EXP + N_EXPERTS
N_SLOTS = N_BLOCKS * TB_EXP
ROUTE_ROWS = 40
NEG_BIG = -1e30
VMEM_LIMIT = 56 * 1024 * 1024


def _cparams(n_axes):
    return pltpu.CompilerParams(dimension_semantics=("arbitrary",) * n_axes,
                                vmem_limit_bytes=VMEM_LIMIT)


def _ada_kernel(c_ref, w_ref, b_ref, o_ref):
    c = c_ref[...]
    ca = (c * jax.nn.sigmoid(c)).astype(BF16)
    o_ref[...] = jnp.dot(ca, w_ref[...].astype(BF16), preferred_element_type=F32) + b_ref[...]


def _ada(c, w_ada, b_ada):
    n_out = w_ada.shape[1]
    tn = 1024
    return pl.pallas_call(
        _ada_kernel,
        grid=(n_out // tn,),
        in_specs=[pl.BlockSpec((BATCH, D_MODEL), lambda j: (0, 0)),
                  pl.BlockSpec((D_MODEL, tn), lambda j: (0, j)),
                  pl.BlockSpec((1, tn), lambda j: (0, j))],
        out_specs=pl.BlockSpec((BATCH, tn), lambda j: (0, j)),
        out_shape=jax.ShapeDtypeStruct((BATCH, n_out), F32),
        compiler_params=_cparams(1),
        name="ada",
    )(c, w_ada, b_ada.reshape(1, n_out))


def _rms_mod(x, g, shift, scale):
    ms = jnp.mean(x * x, axis=-1, keepdims=True)
    y = x * lax.rsqrt(ms + EPS) * g
    return y * (1.0 + scale) + shift


def _inproj_kernel(x_ref, g_ref, sh_ref, sc_ref, w_ref, o_ref):
    h = _rms_mod(x_ref[...], g_ref[...], sh_ref[...], sc_ref[...]).astype(BF16)
    o_ref[:, 0:SB_WIDTH] = (jnp.dot(h, w_ref[:, 0:SB_WIDTH], preferred_element_type=F32) * LOG2_E).astype(BF16)
    for lo in range(SB_WIDTH, MIX_PROJ_WIDTH, SB_WIDTH):
        o_ref[:, lo:lo + SB_WIDTH] = jnp.dot(
            h, w_ref[:, lo:lo + SB_WIDTH], preferred_element_type=F32).astype(BF16)


def _inproj(x2, g1, mods4, w_in_bf):
    tiles_per_seq = SEQ // TM_PROJ
    mod_spec = lambda k: pl.BlockSpec((None, None, 1, D_MODEL),
                                      lambda i: (i // tiles_per_seq, k, 0, 0))
    return pl.pallas_call(
        _inproj_kernel,
        grid=(N_TOK // TM_PROJ,),
        in_specs=[pl.BlockSpec((TM_PROJ, D_MODEL), lambda i: (i, 0)),
                  pl.BlockSpec((1, D_MODEL), lambda i: (0, 0)),
                  mod_spec(0), mod_spec(1),
                  pl.BlockSpec((D_MODEL, MIX_PROJ_WIDTH), lambda i: (0, 0))],
        out_specs=pl.BlockSpec((TM_PROJ, MIX_PROJ_WIDTH), lambda i: (i, 0)),
        out_shape=jax.ShapeDtypeStruct((N_TOK, MIX_PROJ_WIDTH), BF16),
        compiler_params=_cparams(1),
        name="inproj",
    )(x2, g1, mods4, mods4, w_in_bf)


def _attn_kernel(q_ref, k_ref, v_ref, o_ref, z_a, z_b, a_a, a_b, acc_ref, r_ref):
    qi = pl.program_id(2)
    head_q = lax.broadcasted_iota(jnp.int32, (TQ, MXU_DIM), 1) // HEAD_DIM
    head_k = lax.broadcasted_iota(jnp.int32, (TK, MXU_DIM), 1) // HEAD_DIM
    row = lax.broadcasted_iota(jnp.int32, (TQ, TK), 0)
    col = lax.broadcasted_iota(jnp.int32, (TQ, TK), 1)
    past = col < row
    upper = jnp.where(row > col, 1.0, 0.0).astype(BF16)
    heads_per_group = MXU_DIM // HEAD_DIM
    n_groups = ATTN_WIDTH // MXU_DIM

    def group_cols(g):
        return slice(g * MXU_DIM, (g + 1) * MXU_DIM)

    def scores(kb, z_out):
        start = pl.multiple_of(kb * TK, TK)
        for g in range(n_groups):
            qg = q_ref[:, group_cols(g)]
            k = k_ref[pl.ds(start, TK), group_cols(g)]
            for hh in range(heads_per_group):
                q_h = jnp.where(head_q == hh, qg, jnp.zeros_like(qg))
                z_out[g * heads_per_group + hh] = lax.dot_general(
                    q_h, k, (((1,), (1,)), ((), ())), preferred_element_type=F32)

    def weights(z_in, a_out, diag):
        for h in range(ATTN_HEADS):
            z2 = z_in[h]
            sp2 = jnp.maximum(z2, 0.0) + jnp.log(1.0 + jnp.exp2(-jnp.abs(z2))) * LOG2_E
            stay = jnp.where(past, sp2, 0.0) if diag else sp2
            cum = jnp.dot(stay.astype(BF16), upper, preferred_element_type=F32)
            a = jnp.exp2(z2 - (sp2 + cum + r_ref[h]))
            if diag:
                a = jnp.where(past, a, 0.0)
            a_out[h] = a.astype(BF16)
            r_ref[h] = r_ref[h] + jnp.sum(stay, axis=-1, keepdims=True)

    def accumulate(kb, a_in):
        start = pl.multiple_of(kb * TK, TK)
        for g in range(n_groups):
            v = v_ref[pl.ds(start, TK), group_cols(g)]
            acc = acc_ref[g]
            for hh in range(heads_per_group):
                v_h = jnp.where(head_k == hh, v, jnp.zeros_like(v))
                acc = acc + jnp.dot(a_in[g * heads_per_group + hh], v_h, preferred_element_type=F32)
            acc_ref[g] = acc

    acc_ref[...] = jnp.zeros_like(acc_ref)
    r_ref[...] = jnp.zeros_like(r_ref)
    scores(qi, z_a)
    scores(jnp.maximum(qi - 1, 0), z_b)
    weights(z_a, a_a, True)

    def step(carry):
        i, _ = carry
        kb = qi - i
        nxt = jnp.maximum(kb - 1, 0)

        @pl.when(i % 2 == 1)
        def _():
            accumulate(kb + 1, a_a)
            scores(nxt, z_a)
            weights(z_b, a_b, False)

        @pl.when(i % 2 == 0)
        def _():
            accumulate(kb + 1, a_b)
            scores(nxt, z_b)
            weights(z_a, a_a, False)
        return i + 1, jnp.min(r_ref[...]) >= UNDERFLOW_LOG2

    def first_older_block():
        accumulate(qi, a_a)
        weights(z_b, a_b, False)
        return (jnp.min(r_ref[...]) >= UNDERFLOW_LOG2).astype(jnp.int32)

    done = lax.cond(qi >= 1, first_older_block, lambda: jnp.int32(1))
    go_on = jnp.logical_and(qi >= 2, done == 0)

    @pl.when(go_on)
    def _():
        scores(qi - 2, z_a)

    n_done, _ = lax.while_loop(lambda c: jnp.logical_and(c[0] <= qi, jnp.logical_not(c[1])),
                               step, (jnp.where(qi >= 1, 2, 1).astype(jnp.int32), jnp.logical_not(go_on)))
    last = n_done - 1

    @pl.when(last % 2 == 1)
    def _():
        accumulate(qi - last, a_b)

    @pl.when(last % 2 == 0)
    def _():
        accumulate(qi - last, a_a)

    for g in range(n_groups):
        o_ref[:, group_cols(g)] = acc_ref[g].astype(BF16)


def _attention(proj):
    n_q = SEQ // TQ
    n_hg = SB_WIDTH // ATTN_WIDTH
    return pl.pallas_call(
        _attn_kernel,
        grid=(BATCH, n_hg, n_q),
        in_specs=[pl.BlockSpec((TQ, ATTN_WIDTH), lambda b, h, i: (b * n_q + i, h)),
                  pl.BlockSpec((SEQ, ATTN_WIDTH), lambda b, h, i: (b, n_hg + h)),
                  pl.BlockSpec((SEQ, ATTN_WIDTH), lambda b, h, i: (b, 2 * n_hg + h))],
        out_specs=pl.BlockSpec((TQ, ATTN_WIDTH), lambda b, h, i: (b * n_q + i, h)),
        out_shape=jax.ShapeDtypeStruct((N_TOK, SB_WIDTH), BF16),
        scratch_shapes=[pltpu.VMEM((ATTN_HEADS, TQ, TK), F32), pltpu.VMEM((ATTN_HEADS, TQ, TK), F32),
                        pltpu.VMEM((ATTN_HEADS, TQ, TK), BF16), pltpu.VMEM((ATTN_HEADS, TQ, TK), BF16),
                        pltpu.VMEM((ATTN_WIDTH // MXU_DIM, TQ, MXU_DIM), F32),
                        pltpu.VMEM((ATTN_HEADS, TQ, 1), F32)],
        compiler_params=_cparams(3),
        name="attn",
    )(proj, proj, proj)


def _merge_kernel(x_ref, osb_ref, p_ref, halo_ref, gsb_ref, gpool_ref,
                  poolw_ref, pscale_ref, wsb_ref, wpool_ref, wout_ref,
                  gate1_ref, shift2_ref, scale2_ref, g2_ref, wr_ref, br_ref,
                  x1_ref, h2_ref, lt_ref, pext_ref):
    i = pl.program_id(0)
    tiles_per_seq = SEQ // TM_PROJ
    t0 = (i % tiles_per_seq) * TM_PROJ
    halo = halo_ref[...].astype(F32)
    pext_ref[0:HALO, :] = jnp.where(t0 == 0, jnp.zeros_like(halo), halo)
    pext_ref[HALO:, :] = p_ref[...].astype(F32)
    pos = t0 + lax.broadcasted_iota(jnp.int32, (TM_PROJ, 1), 0)
    o_pool = []
    for g, w in enumerate(POOL_WINDOWS):
        cols = slice(g * POOL_GROUP_DIM, (g + 1) * POOL_GROUP_DIM)
        cur = pext_ref[HALO:, cols]
        win = cur
        for d in range(1, w):
            win = win + pext_ref[HALO - d:HALO - d + TM_PROJ, cols]
        count = jnp.minimum(pos + 1, w).astype(F32)
        pooled = win / count - cur
        og = jnp.dot(pooled.astype(BF16), poolw_ref[g], preferred_element_type=F32)
        o_pool.append(og * pscale_ref[:, cols])
    o_pool = jnp.concatenate(o_pool, axis=-1).astype(BF16)
    br_sb = jnp.dot(osb_ref[...], wsb_ref[...], preferred_element_type=F32)
    br_pool = jnp.dot(o_pool, wpool_ref[...], preferred_element_type=F32)
    merged = (jax.nn.sigmoid(gsb_ref[...].astype(F32)) * br_sb
              + jax.nn.sigmoid(gpool_ref[...].astype(F32)) * br_pool)
    mix = jnp.dot(merged.astype(BF16), wout_ref[...], preferred_element_type=F32)
    x1 = x_ref[...] + gate1_ref[...] * mix
    x1_ref[...] = x1
    h2 = _rms_mod(x1, g2_ref[...], shift2_ref[...], scale2_ref[...])
    h2_ref[...] = h2.reshape(TM_PROJ, *ROW_TILE)
    lt_ref[...] = lax.dot_general(wr_ref[...], h2.astype(BF16), (((1,), (1,)), ((), ())),
                                  preferred_element_type=F32) + br_ref[...]


def _merge(x2, o_sb, proj, mods4, g2, poolw_bf, pool_scale, wsb_bf, wpool_bf, wout_bf, wr_bf, br):
    tiles_per_seq = SEQ // TM_PROJ
    halo_blocks = TM_PROJ // HALO
    mod_spec = lambda k: pl.BlockSpec((None, None, 1, D_MODEL),
                                      lambda i: (i // tiles_per_seq, k, 0, 0))
    full = lambda shape: pl.BlockSpec(shape, lambda i: (0,) * len(shape))
    return pl.pallas_call(
        _merge_kernel,
        grid=(N_TOK // TM_PROJ,),
        in_specs=[pl.BlockSpec((TM_PROJ, D_MODEL), lambda i: (i, 0)),
                  pl.BlockSpec((TM_PROJ, SB_WIDTH), lambda i: (i, 0)),
                  pl.BlockSpec((TM_PROJ, POOL_WIDTH), lambda i: (i, 3)),
                  pl.BlockSpec((HALO, POOL_WIDTH),
                               lambda i: (jnp.maximum(i * halo_blocks - 1, 0), 3)),
                  pl.BlockSpec((TM_PROJ, D_MODEL), lambda i: (i, 2)),
                  pl.BlockSpec((TM_PROJ, D_MODEL), lambda i: (i, 3)),
                  full((len(POOL_WINDOWS), POOL_GROUP_DIM, POOL_GROUP_DIM)),
                  full((1, POOL_WIDTH)),
                  full((SB_WIDTH, D_MODEL)),
                  full((POOL_WIDTH, D_MODEL)),
                  full((D_MODEL, D_MODEL)),
                  mod_spec(2), mod_spec(3), mod_spec(4),
                  full((1, D_MODEL)),
                  full((ROUTE_ROWS, D_MODEL)),
                  full((ROUTE_ROWS, 1))],
        out_specs=[pl.BlockSpec((TM_PROJ, D_MODEL), lambda i: (i, 0)),
                   pl.BlockSpec((TM_PROJ, *ROW_TILE), lambda i: (i, 0, 0)),
                   pl.BlockSpec((ROUTE_ROWS, TM_PROJ), lambda i: (0, i))],
        out_shape=[jax.ShapeDtypeStruct((N_TOK, D_MODEL), F32),
                   jax.ShapeDtypeStruct((N_TOK, *ROW_TILE), F32),
                   jax.ShapeDtypeStruct((ROUTE_ROWS, N_TOK), F32)],
        scratch_shapes=[pltpu.VMEM((TM_PROJ + HALO, POOL_WIDTH), F32)],
        compiler_params=_cparams(1),
        name="merge",
    )(x2, o_sb, proj, proj, proj, proj, poolw_bf, pool_scale, wsb_bf, wpool_bf, wout_bf,
      mods4, mods4, mods4, g2, wr_bf, br)


def _first_index_of(mask, idx, big):
    return jnp.min(jnp.where(mask, idx, big), axis=0, keepdims=True)


def _route_kernel(lt_ref, info_ref, slot_ref, cnt_ref, base_ref, info_scr):
    step = pl.program_id(0)
    n_rank_steps = N_TOK // TN_ROUTE
    tn = TN_ROUTE

    @pl.when(step == 0)
    def _():
        base_ref[...] = jnp.zeros_like(base_ref)

    @pl.when(step < n_rank_steps)
    def _():
        tile = pl.ds(pl.multiple_of(step * tn, tn), tn)
        row8 = lax.broadcasted_iota(jnp.int32, (8, tn), 0).astype(F32)
        erow = lax.broadcasted_iota(jnp.int32, (N_EXPERTS, tn), 0).astype(F32)
        gl = lt_ref[0:8, :]
        gmax = jnp.max(gl, axis=0, keepdims=True)
        gsel = _first_index_of(gl == gmax, row8, 8.0)
        p_group = 1.0 / jnp.sum(jnp.exp(gl - gmax), axis=0, keepdims=True)
        ig = lt_ref[8:16, :]
        for g in range(1, N_EXPERT_GROUPS):
            ig = jnp.where(gsel == float(g), lt_ref[8 + 8 * g:16 + 8 * g, :], ig)
        m1 = jnp.max(ig, axis=0, keepdims=True)
        i1 = _first_index_of(ig == m1, row8, 8.0)
        rest = jnp.where(row8 == i1, -jnp.inf, ig)
        m2 = jnp.max(rest, axis=0, keepdims=True)
        i2 = _first_index_of(rest == m2, row8, 8.0)
        e2 = jnp.exp(m2 - m1)
        w_first = p_group / (1.0 + e2)
        w_second = p_group * e2 / (1.0 + e2)
        e_first = gsel * EXPERTS_PER_GROUP + i1
        e_second = gsel * EXPERTS_PER_GROUP + i2
        hit_first = erow == e_first
        hit_second = erow == e_second
        onehot = jnp.where(hit_first | hit_second, 1.0, 0.0)
        r_i = lax.broadcasted_iota(jnp.int32, (tn, tn), 0)
        c_i = lax.broadcasted_iota(jnp.int32, (tn, tn), 1)
        tri = jnp.where(r_i < c_i, 1.0, 0.0).astype(BF16)
        rank = jnp.dot(onehot.astype(BF16), tri, preferred_element_type=F32) + base_ref[...]
        rank_first = jnp.sum(jnp.where(hit_first, rank, 0.0), axis=0, keepdims=True)
        rank_second = jnp.sum(jnp.where(hit_second, rank, 0.0), axis=0, keepdims=True)
        base_ref[...] = base_ref[...] + jnp.sum(onehot, axis=1, keepdims=True)
        info = jnp.zeros((8, tn), F32)
        for r, val in enumerate((e_first, e_second, rank_first, rank_second, w_first, w_second)):
            info = jnp.where(row8 == float(r), val, info)
        info_scr[:, tile] = info

    @pl.when(step >= n_rank_steps)
    def _():
        ts = TN_SLOT
        tile = pl.ds(pl.multiple_of((step - n_rank_steps) * ts, ts), ts)
        row8 = lax.broadcasted_iota(jnp.int32, (8, ts), 0).astype(F32)
        erow = lax.broadcasted_iota(jnp.int32, (N_EXPERTS, ts), 0).astype(F32)
        counts = base_ref[...]
        padded = jnp.floor((counts + (TB_EXP - 1.0)) * (1.0 / TB_EXP)) * TB_EXP
        r_e = lax.broadcasted_iota(jnp.int32, (N_EXPERTS, N_EXPERTS), 0)
        c_e = lax.broadcasted_iota(jnp.int32, (N_EXPERTS, N_EXPERTS), 1)
        padded_row = jnp.sum(jnp.where(r_e == c_e, padded, 0.0), axis=0, keepdims=True)
        pad_start = jnp.sum(jnp.where(c_e < r_e, padded_row, 0.0), axis=1, keepdims=True)
        info = info_scr[:, tile]
        slot_first = jnp.sum(jnp.where(erow == info[0:1, :], pad_start, 0.0), axis=0, keepdims=True) + info[2:3, :]
        slot_second = jnp.sum(jnp.where(erow == info[1:2, :], pad_start, 0.0), axis=0, keepdims=True) + info[3:4, :]
        slots = jnp.where(row8 == 0.0, slot_first, jnp.where(row8 == 1.0, slot_second, 0.0))
        slot_ref[...] = slots.astype(jnp.int32)
        info_ref[...] = info
        cnt_ref[...] = counts


def _route(logits_t):
    n_rank_steps = N_TOK // TN_ROUTE
    n_slot_steps = N_TOK // TN_SLOT
    slot_blk = lambda s: (0, jnp.maximum(s - n_rank_steps, 0))
    return pl.pallas_call(
        _route_kernel,
        grid=(n_rank_steps + n_slot_steps,),
        in_specs=[pl.BlockSpec((ROUTE_ROWS, TN_ROUTE), lambda s: (0, jnp.minimum(s, n_rank_steps - 1)))],
        out_specs=[pl.BlockSpec((8, TN_SLOT), slot_blk),
                   pl.BlockSpec((8, TN_SLOT), slot_blk),
                   pl.BlockSpec((N_EXPERTS, 1), lambda s: (0, 0))],
        out_shape=[jax.ShapeDtypeStruct((8, N_TOK), F32),
                   jax.ShapeDtypeStruct((8, N_TOK), jnp.int32),
                   jax.ShapeDtypeStruct((N_EXPERTS, 1), F32)],
        scratch_shapes=[pltpu.VMEM((N_EXPERTS, 1), F32),
                        pltpu.VMEM((8, N_TOK), F32)],
        compiler_params=_cparams(1),
        name="route",
    )(logits_t)


def _dispatch_kernel(pend_ref, s0_ref, s1_ref, h2_ref, xs_hbm, zero_ref, sem, zero_sem):
    @pl.when(pl.program_id(0) == 0)
    def _():
        zero_ref[...] = jnp.zeros_like(zero_ref)
        clears = []
        for e in range(N_EXPERTS):
            end = pend_ref[e]
            has_block = end > (pend_ref[e - 1] if e > 0 else 0)
            start = pl.multiple_of(jnp.maximum(end - TB_EXP, 0), TB_EXP)
            clears.append((has_block, pltpu.make_async_copy(zero_ref, xs_hbm.at[pl.ds(start, TB_EXP)], zero_sem)))
        for t in range(N_BLOCKS - (2 * N_TOK) // TB_EXP):
            start = pl.multiple_of(jnp.minimum(pend_ref[N_EXPERTS - 1] + t * TB_EXP, N_SLOTS - TB_EXP), TB_EXP)
            unused = pend_ref[N_EXPERTS - 1] + t * TB_EXP < N_SLOTS
            clears.append((unused, pltpu.make_async_copy(zero_ref, xs_hbm.at[pl.ds(start, TB_EXP)], zero_sem)))
        for has_block, copy in clears:
            pl.when(has_block)(copy.start)
        for has_block, copy in clears:
            pl.when(has_block)(copy.wait)

    def row_copy(j, slot):
        return pltpu.make_async_copy(h2_ref.at[j], xs_hbm.at[slot], sem)

    def issue(j, c):
        row_copy(j, s0_ref[j]).start(priority=0)
        row_copy(j, s1_ref[j]).start(priority=1)
        return c

    lax.fori_loop(0, TT_DISP, issue, 0, unroll=8)

    def drain(j, c):
        row_copy(j, 0).wait()
        row_copy(j, 0).wait()
        return c

    lax.fori_loop(0, TT_DISP, drain, 0, unroll=8)


def _dispatch(pad_end, slot0, slot1, h2):
    smem = lambda: pl.BlockSpec((TT_DISP,), lambda i, pe: (i,), memory_space=pltpu.SMEM)
    return pl.pallas_call(
        _dispatch_kernel,
        grid_spec=pltpu.PrefetchScalarGridSpec(
            num_scalar_prefetch=1,
            grid=(N_TOK // TT_DISP,),
            in_specs=[smem(), smem(),
                      pl.BlockSpec((TT_DISP, *ROW_TILE), lambda i, pe: (i, 0, 0))],
            out_specs=pl.BlockSpec(memory_space=pl.ANY),
            scratch_shapes=[pltpu.VMEM((TB_EXP, *ROW_TILE), F32),
                            pltpu.SemaphoreType.DMA(()),
                            pltpu.SemaphoreType.DMA(())]),
        out_shape=jax.ShapeDtypeStruct((N_SLOTS, *ROW_TILE), F32),
        compiler_params=_cparams(1),
        name="dispatch",
    )(pad_end, slot0, slot1, h2)


def _experts_kernel(be_ref, nused_ref, xs_ref, wg_ref, wu_ref, wd_ref, ys_ref):
    used = pl.program_id(0) < nused_ref[0]

    @pl.when(used)
    def _():
        x = xs_ref[...].reshape(TB_EXP, D_MODEL).astype(BF16)
        g = jnp.dot(x, wg_ref[...].astype(BF16), preferred_element_type=F32)
        u = jnp.dot(x, wu_ref[...].astype(BF16), preferred_element_type=F32)
        mid = (g * jax.nn.sigmoid(g) * u).astype(BF16)
        y = jnp.dot(mid, wd_ref[...].astype(BF16), preferred_element_type=F32)
        ys_ref[...] = y.reshape(TB_EXP, *ROW_TILE)

    @pl.when(jnp.logical_not(used))
    def _():
        ys_ref[...] = jnp.zeros_like(ys_ref)


def _experts(block_e, n_used, xs, wg_bf, wu_bf, wd_bf):
    blk = lambda i, be, nu: (jnp.minimum(i, nu[0] - 1), 0, 0)
    wsel = lambda i, be, nu: (be[i], 0, 0)
    return pl.pallas_call(
        _experts_kernel,
        grid_spec=pltpu.PrefetchScalarGridSpec(
            num_scalar_prefetch=2,
            grid=(N_BLOCKS,),
            in_specs=[pl.BlockSpec((TB_EXP, *ROW_TILE), blk),
                      pl.BlockSpec((None, D_MODEL, D_EXPERT), wsel),
                      pl.BlockSpec((None, D_MODEL, D_EXPERT), wsel),
                      pl.BlockSpec((None, D_EXPERT, D_MODEL), wsel)],
            out_specs=pl.BlockSpec((TB_EXP, *ROW_TILE), lambda i, be, nu: (i, 0, 0))),
        out_shape=jax.ShapeDtypeStruct((N_SLOTS, *ROW_TILE), F32),
        compiler_params=_cparams(1),
        name="experts",
    )(block_e, n_used, xs, wg_bf, wu_bf, wd_bf)


def _combine_kernel(s0_ref, s1_ref, s0_next_ref, s1_next_ref, ys_hbm, x1_ref, wts_ref, gate2_ref, gf_ref,
                    o_ref, y_ref, sem):
    i = pl.program_id(0)
    cur = i % 2

    def row_copy(slot, buf, pick, j):
        return pltpu.make_async_copy(ys_hbm.at[slot], y_ref.at[buf, pick, j], sem.at[buf])

    def issue(first_ref, second_ref, buf):
        def body(j, c):
            row_copy(first_ref[j], buf, 0, j).start(priority=0)
            row_copy(second_ref[j], buf, 1, j).start(priority=1)
            return c
        lax.fori_loop(0, TC_COMB, body, 0, unroll=8)

    @pl.when(i == 0)
    def _():
        issue(s0_ref, s1_ref, 0)

    @pl.when(i + 1 < pl.num_programs(0))
    def _():
        issue(s0_next_ref, s1_next_ref, 1 - cur)

    def drain(j, c):
        row_copy(0, cur, 0, j).wait()
        row_copy(0, cur, 1, j).wait()
        return c

    lax.fori_loop(0, TC_COMB, drain, 0, unroll=8)
    wts = wts_ref[...]
    y_first = y_ref[cur, 0].reshape(TC_COMB, D_MODEL)
    y_second = y_ref[cur, 1].reshape(TC_COMB, D_MODEL)
    moe = wts[:, 4:5] * y_first + wts[:, 5:6] * y_second
    x2 = x1_ref[...] + gate2_ref[...] * moe
    ms = jnp.mean(x2 * x2, axis=-1, keepdims=True)
    o_ref[...] = x2 * lax.rsqrt(ms + EPS) * gf_ref[...]


def _combine(slot0, slot1, ys, x1, wts_rows, mods4, gf):
    n_steps = N_TOK // TC_COMB
    tiles_per_seq = SEQ // TC_COMB
    smem = lambda: pl.BlockSpec((TC_COMB,), lambda i: (i,), memory_space=pltpu.SMEM)
    smem_next = lambda: pl.BlockSpec((TC_COMB,), lambda i: (jnp.minimum(i + 1, n_steps - 1),),
                                     memory_space=pltpu.SMEM)
    return pl.pallas_call(
        _combine_kernel,
        grid=(n_steps,),
        in_specs=[smem(), smem(), smem_next(), smem_next(),
                  pl.BlockSpec(memory_space=pl.ANY),
                  pl.BlockSpec((TC_COMB, D_MODEL), lambda i: (i, 0)),
                  pl.BlockSpec((TC_COMB, 8), lambda i: (i, 0)),
                  pl.BlockSpec((None, None, 1, D_MODEL), lambda i: (i // tiles_per_seq, 5, 0, 0)),
                  pl.BlockSpec((1, D_MODEL), lambda i: (0, 0))],
        out_specs=pl.BlockSpec((TC_COMB, D_MODEL), lambda i: (i, 0)),
        out_shape=jax.ShapeDtypeStruct((N_TOK, D_MODEL), F32),
        scratch_shapes=[pltpu.VMEM((2, 2, TC_COMB, *ROW_TILE), F32),
                        pltpu.SemaphoreType.DMA((2,))],
        compiler_params=_cparams(1),
        name="combine",
    )(slot0, slot1, slot0, slot1, ys, x1, wts_rows, mods4, gf)


def _layer(x2, mods4, norm1_g, norm2_g, w_in, pool_w, pool_scale, w_branch_sb, w_branch_pool,
           w_out, w_rg, b_rg, w_re, b_re, w_gate, w_up, w_down, norm_f_g):
    q_scale = jnp.where(jnp.arange(MIX_PROJ_WIDTH) < SB_WIDTH, HEAD_DIM ** -0.5, 1.0).astype(F32)
    w_in_bf = (w_in * q_scale[None, :]).astype(BF16)
    proj = _inproj(x2, norm1_g.reshape(1, D_MODEL), mods4, w_in_bf)
    o_sb = _attention(proj)

    wr = jnp.concatenate([w_rg.T, jnp.zeros((4, D_MODEL), F32),
                          w_re.transpose(0, 2, 1).reshape(N_EXPERTS, D_MODEL)], axis=0)
    br = jnp.concatenate([b_rg, jnp.full((4,), NEG_BIG, F32), b_re.reshape(N_EXPERTS)]).reshape(ROUTE_ROWS, 1)
    x1, h2, logits_t = _merge(
        x2, o_sb, proj, mods4, norm2_g.reshape(1, D_MODEL), pool_w.astype(BF16),
        pool_scale.reshape(1, POOL_WIDTH), w_branch_sb.astype(BF16), w_branch_pool.astype(BF16),
        w_out.astype(BF16), wr.astype(BF16), br)

    info, slots, counts = _route(logits_t)
    slot0, slot1 = slots[0], slots[1]
    counts = counts.reshape(N_EXPERTS).astype(jnp.int32)
    pad_end = jnp.cumsum(((counts + TB_EXP - 1) // TB_EXP) * TB_EXP)
    block_first = jnp.arange(N_BLOCKS, dtype=jnp.int32) * TB_EXP
    block_e = jnp.minimum(jnp.sum((pad_end[None, :] <= block_first[:, None]).astype(jnp.int32), axis=1),
                          N_EXPERTS - 1).astype(jnp.int32)
    n_used = (pad_end[-1:] // TB_EXP).astype(jnp.int32)

    xs = _dispatch(pad_end.astype(jnp.int32), slot0, slot1, h2)
    ys = _experts(block_e, n_used, xs, w_gate, w_up, w_down)
    return _combine(slot0, slot1, ys, x1, info.T, mods4, norm_f_g.reshape(1, D_MODEL))


def kernel(x, c, norm1_g, norm2_g, w_ada, b_ada, w_in, pool_w, pool_scale, w_branch_sb, w_branch_pool, w_out, w_route_group, b_route_group, w_route_expert, b_route_expert, w_exp_gate, w_exp_up, w_exp_down, norm_f_g):
    assert x.shape == (BATCH, SEQ, D_MODEL) and w_ada.shape[0] == 1
    mods = _ada(c, w_ada[0], b_ada[0])
    mods4 = mods.reshape(BATCH, 6, 1, D_MODEL)
    out = _layer(x.reshape(N_TOK, D_MODEL), mods4, norm1_g[0], norm2_g[0], w_in[0], pool_w[0],
                 pool_scale[0], w_branch_sb[0], w_branch_pool[0], w_out[0], w_route_group[0],
                 b_route_group[0], w_route_expert[0], b_route_expert[0], w_exp_gate[0], w_exp_up[0],
                 w_exp_down[0], norm_f_g)
    return out.reshape(BATCH, SEQ, D_MODEL)
```

```python
import jax
import jax.numpy as jnp
from jax import lax
from jax.experimental import pallas as pl
from jax.experimental.pallas import tpu as pltpu

F32 = jnp.float32
BF16 = jnp.bfloat16

D_MODEL = 1024
BATCH = 16
SEQ = 2048
N_TOK = BATCH * SEQ
HEAD_DIM = 64
SB_WIDTH = 512
POOL_WIDTH = 512
POOL_WINDOWS = (2, 4, 8, 16)
POOL_GROUP_DIM = 128
MIX_PROJ_WIDTH = 4096
N_EXPERT_GROUPS = 4
EXPERTS_PER_GROUP = 8
N_EXPERTS = 32
D_EXPERT = 512
EPS = 1e-6

LANES = 128
SUBLANES = 8
ROW_TILE = (SUBLANES, D_MODEL // SUBLANES)
assert ROW_TILE[1] == LANES
HALO = 16
TM_PROJ = 512
TQ = 256
TK = 256
MXU_DIM = 256
ATTN_HEADS = 8
ATTN_WIDTH = ATTN_HEADS * HEAD_DIM
LOG2_E = 1.4426950408889634
UNDERFLOW_LOG2 = 160.0
TN_ROUTE = 512
TN_SLOT = 4096
TT_DISP = 1024
TB_EXP = 512
TC_COMB = 256
N_BLOCKS = (2 * N_TOK) // TB_EXP + N_EXPERTS
N_SLOTS = N_BLOCKS * TB_EXP
ROUTE_ROWS = 40
NEG_BIG = -1e30
V7X_VMEM_BYTES = 64 * 1024 * 1024
VMEM_LIMIT = V7X_VMEM_BYTES - 8 * 1024 * 1024


def _cparams(n_axes):
    return pltpu.CompilerParams(dimension_semantics=("arbitrary",) * n_axes,
                                vmem_limit_bytes=VMEM_LIMIT)


def _ada_kernel(c_ref, w_ref, b_ref, o_ref):
    c = c_ref[...]
    ca = (c * jax.nn.sigmoid(c)).astype(BF16)
    o_ref[...] = jnp.dot(ca, w_ref[...].astype(BF16), preferred_element_type=F32) + b_ref[...]


def _ada(c, w_ada, b_ada):
    n_out = w_ada.shape[1]
    tn = 1024
    return pl.pallas_call(
        _ada_kernel,
        grid=(n_out // tn,),
        in_specs=[pl.BlockSpec((BATCH, D_MODEL), lambda j: (0, 0)),
                  pl.BlockSpec((D_MODEL, tn), lambda j: (0, j)),
                  pl.BlockSpec((1, tn), lambda j: (0, j))],
        out_specs=pl.BlockSpec((BATCH, tn), lambda j: (0, j)),
        out_shape=jax.ShapeDtypeStruct((BATCH, n_out), F32),
        compiler_params=_cparams(1),
        name="ada",
    )(c, w_ada, b_ada.reshape(1, n_out))


def _rms_mod(x, g, shift, scale):
    ms = jnp.mean(x * x, axis=-1, keepdims=True)
    y = x * lax.rsqrt(ms + EPS) * g
    return y * (1.0 + scale) + shift


def _inproj_kernel(x_ref, g_ref, sh_ref, sc_ref, w_ref, o_ref):
    h = _rms_mod(x_ref[...], g_ref[...], sh_ref[...], sc_ref[...]).astype(BF16)
    o_ref[:, 0:SB_WIDTH] = (jnp.dot(h, w_ref[:, 0:SB_WIDTH], preferred_element_type=F32) * LOG2_E).astype(BF16)
    for lo in range(SB_WIDTH, MIX_PROJ_WIDTH, SB_WIDTH):
        o_ref[:, lo:lo + SB_WIDTH] = jnp.dot(
            h, w_ref[:, lo:lo + SB_WIDTH], preferred_element_type=F32).astype(BF16)


def _inproj(x2, g1, mods4, w_in_bf):
    tiles_per_seq = SEQ // TM_PROJ
    mod_spec = lambda k: pl.BlockSpec((None, None, 1, D_MODEL),
                                      lambda i: (i // tiles_per_seq, k, 0, 0))
    return pl.pallas_call(
        _inproj_kernel,
        grid=(N_TOK // TM_PROJ,),
        in_specs=[pl.BlockSpec((TM_PROJ, D_MODEL), lambda i: (i, 0)),
                  pl.BlockSpec((1, D_MODEL), lambda i: (0, 0)),
                  mod_spec(0), mod_spec(1),
                  pl.BlockSpec((D_MODEL, MIX_PROJ_WIDTH), lambda i: (0, 0))],
        out_specs=pl.BlockSpec((TM_PROJ, MIX_PROJ_WIDTH), lambda i: (i, 0)),
        out_shape=jax.ShapeDtypeStruct((N_TOK, MIX_PROJ_WIDTH), BF16),
        compiler_params=_cparams(1),
        name="inproj",
    )(x2, g1, mods4, mods4, w_in_bf)


def _attn_kernel(q_ref, k_ref, v_ref, o_ref, z_a, z_b, a_a, a_b, acc_ref, r_ref):
    qi = pl.program_id(2)
    head_q = lax.broadcasted_iota(jnp.int32, (TQ, MXU_DIM), 1) // HEAD_DIM
    head_k = lax.broadcasted_iota(jnp.int32, (TK, MXU_DIM), 1) // HEAD_DIM
    row = lax.broadcasted_iota(jnp.int32, (TQ, TK), 0)
    col = lax.broadcasted_iota(jnp.int32, (TQ, TK), 1)
    past = col < row
    upper = jnp.where(row > col, 1.0, 0.0).astype(BF16)
    heads_per_group = MXU_DIM // HEAD_DIM
    n_groups = ATTN_WIDTH // MXU_DIM

    def group_cols(g):
        return slice(g * MXU_DIM, (g + 1) * MXU_DIM)

    def scores(kb, z_out):
        start = pl.multiple_of(kb * TK, TK)
        for g in range(n_groups):
            qg = q_ref[:, group_cols(g)]
            k = k_ref[pl.ds(start, TK), group_cols(g)]
            for hh in range(heads_per_group):
                q_h = jnp.where(head_q == hh, qg, jnp.zeros_like(qg))
                z_out[g * heads_per_group + hh] = lax.dot_general(
                    q_h, k, (((1,), (1,)), ((), ())), preferred_element_type=F32)

    def weights(z_in, a_out, diag):
        for h in range(ATTN_HEADS):
            z2 = z_in[h]
            sp2 = jnp.maximum(z2, 0.0) + jnp.log(1.0 + jnp.exp2(-jnp.abs(z2))) * LOG2_E
            stay = jnp.where(past, sp2, 0.0) if diag else sp2
            cum = jnp.dot(stay.astype(BF16), upper, preferred_element_type=F32)
            a = jnp.exp2(z2 - (sp2 + cum + r_ref[h]))
            if diag:
                a = jnp.where(past, a, 0.0)
            a_out[h] = a.astype(BF16)
            r_ref[h] = r_ref[h] + jnp.sum(stay, axis=-1, keepdims=True)

    def accumulate(kb, a_in):
        start = pl.multiple_of(kb * TK, TK)
        for g in range(n_groups):
            v = v_ref[pl.ds(start, TK), group_cols(g)]
            acc = acc_ref[g]
            for hh in range(heads_per_group):
                v_h = jnp.where(head_k == hh, v, jnp.zeros_like(v))
                acc = acc + jnp.dot(a_in[g * heads_per_group + hh], v_h, preferred_element_type=F32)
            acc_ref[g] = acc

    acc_ref[...] = jnp.zeros_like(acc_ref)
    r_ref[...] = jnp.zeros_like(r_ref)
    scores(qi, z_a)
    scores(jnp.maximum(qi - 1, 0), z_b)
    weights(z_a, a_a, True)

    def step(carry):
        i, _ = carry
        kb = qi - i
        nxt = jnp.maximum(kb - 1, 0)

        @pl.when(i % 2 == 1)
        def _():
            accumulate(kb + 1, a_a)
            scores(nxt, z_a)
            weights(z_b, a_b, False)

        @pl.when(i % 2 == 0)
        def _():
            accumulate(kb + 1, a_b)
            scores(nxt, z_b)
            weights(z_a, a_a, False)
        return i + 1, jnp.min(r_ref[...]) >= UNDERFLOW_LOG2

    def first_older_block():
        accumulate(qi, a_a)
        weights(z_b, a_b, False)
        return (jnp.min(r_ref[...]) >= UNDERFLOW_LOG2).astype(jnp.int32)

    done = lax.cond(qi >= 1, first_older_block, lambda: jnp.int32(1))
    go_on = jnp.logical_and(qi >= 2, done == 0)

    @pl.when(go_on)
    def _():
        scores(qi - 2, z_a)

    n_done, _ = lax.while_loop(lambda c: jnp.logical_and(c[0] <= qi, jnp.logical_not(c[1])),
                               step, (jnp.where(qi >= 1, 2, 1).astype(jnp.int32), jnp.logical_not(go_on)))
    last = n_done - 1

    @pl.when(last % 2 == 1)
    def _():
        accumulate(qi - last, a_b)

    @pl.when(last % 2 == 0)
    def _():
        accumulate(qi - last, a_a)

    for g in range(n_groups):
        o_ref[:, group_cols(g)] = acc_ref[g].astype(BF16)


def _attention(proj):
    n_q = SEQ // TQ
    n_hg = SB_WIDTH // ATTN_WIDTH
    return pl.pallas_call(
        _attn_kernel,
        grid=(BATCH, n_hg, n_q),
        in_specs=[pl.BlockSpec((TQ, ATTN_WIDTH), lambda b, h, i: (b * n_q + i, h)),
                  pl.BlockSpec((SEQ, ATTN_WIDTH), lambda b, h, i: (b, n_hg + h)),
                  pl.BlockSpec((SEQ, ATTN_WIDTH), lambda b, h, i: (b, 2 * n_hg + h))],
        out_specs=pl.BlockSpec((TQ, ATTN_WIDTH), lambda b, h, i: (b * n_q + i, h)),
        out_shape=jax.ShapeDtypeStruct((N_TOK, SB_WIDTH), BF16),
        scratch_shapes=[pltpu.VMEM((ATTN_HEADS, TQ, TK), F32), pltpu.VMEM((ATTN_HEADS, TQ, TK), F32),
                        pltpu.VMEM((ATTN_HEADS, TQ, TK), BF16), pltpu.VMEM((ATTN_HEADS, TQ, TK), BF16),
                        pltpu.VMEM((ATTN_WIDTH // MXU_DIM, TQ, MXU_DIM), F32),
                        pltpu.VMEM((ATTN_HEADS, TQ, 1), F32)],
        compiler_params=_cparams(3),
        name="attn",
    )(proj, proj, proj)


def _merge_kernel(x_ref, osb_ref, p_ref, halo_ref, gsb_ref, gpool_ref,
                  poolw_ref, pscale_ref, wsb_ref, wpool_ref, wout_ref,
                  gate1_ref, shift2_ref, scale2_ref, g2_ref, wr_ref, br_ref,
                  x1_ref, h2_ref, lt_ref, pext_ref):
    i = pl.program_id(0)
    tiles_per_seq = SEQ // TM_PROJ
    t0 = (i % tiles_per_seq) * TM_PROJ
    halo = halo_ref[...].astype(F32)
    pext_ref[0:HALO, :] = jnp.where(t0 == 0, jnp.zeros_like(halo), halo)
    pext_ref[HALO:, :] = p_ref[...].astype(F32)
    pos = t0 + lax.broadcasted_iota(jnp.int32, (TM_PROJ, 1), 0)
    o_pool = []
    for g, w in enumerate(POOL_WINDOWS):
        cols = slice(g * POOL_GROUP_DIM, (g + 1) * POOL_GROUP_DIM)
        cur = pext_ref[HALO:, cols]
        win = cur
        for d in range(1, w):
            win = win + pext_ref[HALO - d:HALO - d + TM_PROJ, cols]
        count = jnp.minimum(pos + 1, w).astype(F32)
        pooled = win / count - cur
        og = jnp.dot(pooled.astype(BF16), poolw_ref[g], preferred_element_type=F32)
        o_pool.append(og * pscale_ref[:, cols])
    o_pool = jnp.concatenate(o_pool, axis=-1).astype(BF16)
    br_sb = jnp.dot(osb_ref[...], wsb_ref[...], preferred_element_type=F32)
    br_pool = jnp.dot(o_pool, wpool_ref[...], preferred_element_type=F32)
    merged = (jax.nn.sigmoid(gsb_ref[...].astype(F32)) * br_sb
              + jax.nn.sigmoid(gpool_ref[...].astype(F32)) * br_pool)
    mix = jnp.dot(merged.astype(BF16), wout_ref[...], preferred_element_type=F32)
    x1 = x_ref[...] + gate1_ref[...] * mix
    x1_ref[...] = x1
    h2 = _rms_mod(x1, g2_ref[...], shift2_ref[...], scale2_ref[...])
    h2_ref[...] = h2.reshape(TM_PROJ, *ROW_TILE)
    lt_ref[...] = lax.dot_general(wr_ref[...], h2.astype(BF16), (((1,), (1,)), ((), ())),
                                  preferred_element_type=F32) + br_ref[...]


def _merge(x2, o_sb, proj, mods4, g2, poolw_bf, pool_scale, wsb_bf, wpool_bf, wout_bf, wr_bf, br):
    tiles_per_seq = SEQ // TM_PROJ
    halo_blocks = TM_PROJ // HALO
    mod_spec = lambda k: pl.BlockSpec((None, None, 1, D_MODEL),
                                      lambda i: (i // tiles_per_seq, k, 0, 0))
    full = lambda shape: pl.BlockSpec(shape, lambda i: (0,) * len(shape))
    return pl.pallas_call(
        _merge_kernel,
        grid=(N_TOK // TM_PROJ,),
        in_specs=[pl.BlockSpec((TM_PROJ, D_MODEL), lambda i: (i, 0)),
                  pl.BlockSpec((TM_PROJ, SB_WIDTH), lambda i: (i, 0)),
                  pl.BlockSpec((TM_PROJ, POOL_WIDTH), lambda i: (i, 3)),
                  pl.BlockSpec((HALO, POOL_WIDTH),
                               lambda i: (jnp.maximum(i * halo_blocks - 1, 0), 3)),
                  pl.BlockSpec((TM_PROJ, D_MODEL), lambda i: (i, 2)),
                  pl.BlockSpec((TM_PROJ, D_MODEL), lambda i: (i, 3)),
                  full((len(POOL_WINDOWS), POOL_GROUP_DIM, POOL_GROUP_DIM)),
                  full((1, POOL_WIDTH)),
                  full((SB_WIDTH, D_MODEL)),
                  full((POOL_WIDTH, D_MODEL)),
                  full((D_MODEL, D_MODEL)),
                  mod_spec(2), mod_spec(3), mod_spec(4),
                  full((1, D_MODEL)),
                  full((ROUTE_ROWS, D_MODEL)),
                  full((ROUTE_ROWS, 1))],
        out_specs=[pl.BlockSpec((TM_PROJ, D_MODEL), lambda i: (i, 0)),
                   pl.BlockSpec((TM_PROJ, *ROW_TILE), lambda i: (i, 0, 0)),
                   pl.BlockSpec((ROUTE_ROWS, TM_PROJ), lambda i: (0, i))],
        out_shape=[jax.ShapeDtypeStruct((N_TOK, D_MODEL), F32),
                   jax.ShapeDtypeStruct((N_TOK, *ROW_TILE), F32),
                   jax.ShapeDtypeStruct((ROUTE_ROWS, N_TOK), F32)],
        scratch_shapes=[pltpu.VMEM((TM_PROJ + HALO, POOL_WIDTH), F32)],
        compiler_params=_cparams(1),
        name="merge",
    )(x2, o_sb, proj, proj, proj, proj, poolw_bf, pool_scale, wsb_bf, wpool_bf, wout_bf,
      mods4, mods4, mods4, g2, wr_bf, br)


def _first_index_of(mask, idx, big):
    return jnp.min(jnp.where(mask, idx, big), axis=0, keepdims=True)


def _route_kernel(lt_ref, info_ref, slot_ref, cnt_ref, base_ref, info_scr):
    step = pl.program_id(0)
    n_rank_steps = N_TOK // TN_ROUTE
    tn = TN_ROUTE

    @pl.when(step == 0)
    def _():
        base_ref[...] = jnp.zeros_like(base_ref)

    @pl.when(step < n_rank_steps)
    def _():
        tile = pl.ds(pl.multiple_of(step * tn, tn), tn)
        row8 = lax.broadcasted_iota(jnp.int32, (8, tn), 0).astype(F32)
        erow = lax.broadcasted_iota(jnp.int32, (N_EXPERTS, tn), 0).astype(F32)
        gl = lt_ref[0:8, :]
        gmax = jnp.max(gl, axis=0, keepdims=True)
        gsel = _first_index_of(gl == gmax, row8, 8.0)
        p_group = 1.0 / jnp.sum(jnp.exp(gl - gmax), axis=0, keepdims=True)
        ig = lt_ref[8:16, :]
        for g in range(1, N_EXPERT_GROUPS):
            ig = jnp.where(gsel == float(g), lt_ref[8 + 8 * g:16 + 8 * g, :], ig)
        m1 = jnp.max(ig, axis=0, keepdims=True)
        i1 = _first_index_of(ig == m1, row8, 8.0)
        rest = jnp.where(row8 == i1, -jnp.inf, ig)
        m2 = jnp.max(rest, axis=0, keepdims=True)
        i2 = _first_index_of(rest == m2, row8, 8.0)
        e2 = jnp.exp(m2 - m1)
        w_first = p_group / (1.0 + e2)
        w_second = p_group * e2 / (1.0 + e2)
        e_first = gsel * EXPERTS_PER_GROUP + i1
        e_second = gsel * EXPERTS_PER_GROUP + i2
        hit_first = erow == e_first
        hit_second = erow == e_second
        onehot = jnp.where(hit_first | hit_second, 1.0, 0.0)
        r_i = lax.broadcasted_iota(jnp.int32, (tn, tn), 0)
        c_i = lax.broadcasted_iota(jnp.int32, (tn, tn), 1)
        tri = jnp.where(r_i < c_i, 1.0, 0.0).astype(BF16)
        rank = jnp.dot(onehot.astype(BF16), tri, preferred_element_type=F32) + base_ref[...]
        rank_first = jnp.sum(jnp.where(hit_first, rank, 0.0), axis=0, keepdims=True)
        rank_second = jnp.sum(jnp.where(hit_second, rank, 0.0), axis=0, keepdims=True)
        base_ref[...] = base_ref[...] + jnp.sum(onehot, axis=1, keepdims=True)
        info = jnp.zeros((8, tn), F32)
        for r, val in enumerate((e_first, e_second, rank_first, rank_second, w_first, w_second)):
            info = jnp.where(row8 == float(r), val, info)
        info_scr[:, tile] = info

    @pl.when(step >= n_rank_steps)
    def _():
        ts = TN_SLOT
        tile = pl.ds(pl.multiple_of((step - n_rank_steps) * ts, ts), ts)
        row8 = lax.broadcasted_iota(jnp.int32, (8, ts), 0).astype(F32)
        erow = lax.broadcasted_iota(jnp.int32, (N_EXPERTS, ts), 0).astype(F32)
        counts = base_ref[...]
        padded = jnp.floor((counts + (TB_EXP - 1.0)) * (1.0 / TB_EXP)) * TB_EXP
        r_e = lax.broadcasted_iota(jnp.int32, (N_EXPERTS, N_EXPERTS), 0)
        c_e = lax.broadcasted_iota(jnp.int32, (N_EXPERTS, N_EXPERTS), 1)
        padded_row = jnp.sum(jnp.where(r_e == c_e, padded, 0.0), axis=0, keepdims=True)
        pad_start = jnp.sum(jnp.where(c_e < r_e, padded_row, 0.0), axis=1, keepdims=True)
        info = info_scr[:, tile]
        slot_first = jnp.sum(jnp.where(erow == info[0:1, :], pad_start, 0.0), axis=0, keepdims=True) + info[2:3, :]
        slot_second = jnp.sum(jnp.where(erow == info[1:2, :], pad_start, 0.0), axis=0, keepdims=True) + info[3:4, :]
        slots = jnp.where(row8 == 0.0, slot_first, jnp.where(row8 == 1.0, slot_second, 0.0))
        slot_ref[...] = slots.astype(jnp.int32)
        info_ref[...] = info
        cnt_ref[...] = counts


def _route(logits_t):
    n_rank_steps = N_TOK // TN_ROUTE
    n_slot_steps = N_TOK // TN_SLOT
    slot_blk = lambda s: (0, jnp.maximum(s - n_rank_steps, 0))
    return pl.pallas_call(
        _route_kernel,
        grid=(n_rank_steps + n_slot_steps,),
        in_specs=[pl.BlockSpec((ROUTE_ROWS, TN_ROUTE), lambda s: (0, jnp.minimum(s, n_rank_steps - 1)))],
        out_specs=[pl.BlockSpec((8, TN_SLOT), slot_blk),
                   pl.BlockSpec((8, TN_SLOT), slot_blk),
                   pl.BlockSpec((N_EXPERTS, 1), lambda s: (0, 0))],
        out_shape=[jax.ShapeDtypeStruct((8, N_TOK), F32),
                   jax.ShapeDtypeStruct((8, N_TOK), jnp.int32),
                   jax.ShapeDtypeStruct((N_EXPERTS, 1), F32)],
        scratch_shapes=[pltpu.VMEM((N_EXPERTS, 1), F32),
                        pltpu.VMEM((8, N_TOK), F32)],
        compiler_params=_cparams(1),
        name="route",
    )(logits_t)


def _dispatch_kernel(pend_ref, s0_ref, s1_ref, h2_ref, xs_hbm, zero_ref, sem, zero_sem):
    @pl.when(pl.program_id(0) == 0)
    def _():
        zero_ref[...] = jnp.zeros_like(zero_ref)
        clears = []
        for e in range(N_EXPERTS):
            end = pend_ref[e]
            has_block = end > (pend_ref[e - 1] if e > 0 else 0)
            start = pl.multiple_of(jnp.maximum(end - TB_EXP, 0), TB_EXP)
            clears.append((has_block, pltpu.make_async_copy(zero_ref, xs_hbm.at[pl.ds(start, TB_EXP)], zero_sem)))
        for t in range(N_BLOCKS - (2 * N_TOK) // TB_EXP):
            start = pl.multiple_of(jnp.minimum(pend_ref[N_EXPERTS - 1] + t * TB_EXP, N_SLOTS - TB_EXP), TB_EXP)
            unused = pend_ref[N_EXPERTS - 1] + t * TB_EXP < N_SLOTS
            clears.append((unused, pltpu.make_async_copy(zero_ref, xs_hbm.at[pl.ds(start, TB_EXP)], zero_sem)))
        for has_block, copy in clears:
            pl.when(has_block)(copy.start)
        for has_block, copy in clears:
            pl.when(has_block)(copy.wait)

    def row_copy(j, slot):
        return pltpu.make_async_copy(h2_ref.at[j], xs_hbm.at[slot], sem)

    def issue(j, c):
        row_copy(j, s0_ref[j]).start(priority=0)
        row_copy(j, s1_ref[j]).start(priority=1)
        return c

    lax.fori_loop(0, TT_DISP, issue, 0, unroll=8)

    def drain(j, c):
        row_copy(j, 0).wait()
        row_copy(j, 0).wait()
        return c

    lax.fori_loop(0, TT_DISP, drain, 0, unroll=8)


def _dispatch(pad_end, slot0, slot1, h2):
    smem = lambda: pl.BlockSpec((TT_DISP,), lambda i, pe: (i,), memory_space=pltpu.SMEM)
    return pl.pallas_call(
        _dispatch_kernel,
        grid_spec=pltpu.PrefetchScalarGridSpec(
            num_scalar_prefetch=1,
            grid=(N_TOK // TT_DISP,),
            in_specs=[smem(), smem(),
                      pl.BlockSpec((TT_DISP, *ROW_TILE), lambda i, pe: (i, 0, 0))],
            out_specs=pl.BlockSpec(memory_space=pl.ANY),
            scratch_shapes=[pltpu.VMEM((TB_EXP, *ROW_TILE), F32),
                            pltpu.SemaphoreType.DMA(()),
                            pltpu.SemaphoreType.DMA(())]),
        out_shape=jax.ShapeDtypeStruct((N_SLOTS, *ROW_TILE), F32),
        compiler_params=_cparams(1),
        name="dispatch",
    )(pad_end, slot0, slot1, h2)


def _experts_kernel(be_ref, nused_ref, xs_ref, wg_ref, wu_ref, wd_ref, ys_ref, wg_bf, wu_bf, wd_bf):
    i = pl.program_id(0)
    used = i < nused_ref[0]
    new_expert = jnp.logical_or(i == 0, be_ref[i] != be_ref[jnp.maximum(i - 1, 0)])

    @pl.when(jnp.logical_and(used, new_expert))
    def _():
        wg_bf[...] = wg_ref[...].astype(BF16)
        wu_bf[...] = wu_ref[...].astype(BF16)
        wd_bf[...] = wd_ref[...].astype(BF16)

    @pl.when(used)
    def _():
        x = xs_ref[...].reshape(TB_EXP, D_MODEL).astype(BF16)
        g = jnp.dot(x, wg_bf[...], preferred_element_type=F32)
        u = jnp.dot(x, wu_bf[...], preferred_element_type=F32)
        mid = (g * jax.nn.sigmoid(g) * u).astype(BF16)
        y = jnp.dot(mid, wd_bf[...], preferred_element_type=F32)
        ys_ref[...] = y.reshape(TB_EXP, *ROW_TILE)

    @pl.when(jnp.logical_not(used))
    def _():
        ys_ref[...] = jnp.zeros_like(ys_ref)


def _experts(block_e, n_used, xs, w_gate, w_up, w_down):
    blk = lambda i, be, nu: (jnp.minimum(i, nu[0] - 1), 0, 0)
    wsel = lambda i, be, nu: (be[i], 0, 0)
    return pl.pallas_call(
        _experts_kernel,
        grid_spec=pltpu.PrefetchScalarGridSpec(
            num_scalar_prefetch=2,
            grid=(N_BLOCKS,),
            in_specs=[pl.BlockSpec((TB_EXP, *ROW_TILE), blk),
                      pl.BlockSpec((None, D_MODEL, D_EXPERT), wsel),
                      pl.BlockSpec((None, D_MODEL, D_EXPERT), wsel),
                      pl.BlockSpec((None, D_EXPERT, D_MODEL), wsel)],
            out_specs=pl.BlockSpec((TB_EXP, *ROW_TILE), lambda i, be, nu: (i, 0, 0)),
            scratch_shapes=[pltpu.VMEM((D_MODEL, D_EXPERT), BF16),
                            pltpu.VMEM((D_MODEL, D_EXPERT), BF16),
                            pltpu.VMEM((D_EXPERT, D_MODEL), BF16)]),
        out_shape=jax.ShapeDtypeStruct((N_SLOTS, *ROW_TILE), F32),
        compiler_params=_cparams(1),
        name="experts",
    )(block_e, n_used, xs, w_gate, w_up, w_down)


def _combine_kernel(s0_ref, s1_ref, s0_next_ref, s1_next_ref, ys_hbm, x1_ref, wts_ref, gate2_ref, gf_ref,
                    o_ref, y_ref, sem):
    i = pl.program_id(0)
    cur = i % 2

    def row_copy(slot, buf, pick, j):
        return pltpu.make_async_copy(ys_hbm.at[slot], y_ref.at[buf, pick, j], sem.at[buf])

    def issue(first_ref, second_ref, buf):
        def body(j, c):
            row_copy(first_ref[j], buf, 0, j).start(priority=0)
            row_copy(second_ref[j], buf, 1, j).start(priority=1)
            return c
        lax.fori_loop(0, TC_COMB, body, 0, unroll=8)

    @pl.when(i == 0)
    def _():
        issue(s0_ref, s1_ref, 0)

    @pl.when(i + 1 < pl.num_programs(0))
    def _():
        issue(s0_next_ref, s1_next_ref, 1 - cur)

    def drain(j, c):
        row_copy(0, cur, 0, j).wait()
        row_copy(0, cur, 1, j).wait()
        return c

    lax.fori_loop(0, TC_COMB, drain, 0, unroll=8)
    wts = wts_ref[...]
    y_first = y_ref[cur, 0].reshape(TC_COMB, D_MODEL)
    y_second = y_ref[cur, 1].reshape(TC_COMB, D_MODEL)
    moe = wts[:, 4:5] * y_first + wts[:, 5:6] * y_second
    x2 = x1_ref[...] + gate2_ref[...] * moe
    ms = jnp.mean(x2 * x2, axis=-1, keepdims=True)
    o_ref[...] = x2 * lax.rsqrt(ms + EPS) * gf_ref[...]


def _combine(slot0, slot1, ys, x1, wts_rows, mods4, gf):
    n_steps = N_TOK // TC_COMB
    tiles_per_seq = SEQ // TC_COMB
    smem = lambda: pl.BlockSpec((TC_COMB,), lambda i: (i,), memory_space=pltpu.SMEM)
    smem_next = lambda: pl.BlockSpec((TC_COMB,), lambda i: (jnp.minimum(i + 1, n_steps - 1),),
                                     memory_space=pltpu.SMEM)
    return pl.pallas_call(
        _combine_kernel,
        grid=(n_steps,),
        in_specs=[smem(), smem(), smem_next(), smem_next(),
                  pl.BlockSpec(memory_space=pl.ANY),
                  pl.BlockSpec((TC_COMB, D_MODEL), lambda i: (i, 0)),
                  pl.BlockSpec((TC_COMB, 8), lambda i: (i, 0)),
                  pl.BlockSpec((None, None, 1, D_MODEL), lambda i: (i // tiles_per_seq, 5, 0, 0)),
                  pl.BlockSpec((1, D_MODEL), lambda i: (0, 0))],
        out_specs=pl.BlockSpec((TC_COMB, D_MODEL), lambda i: (i, 0)),
        out_shape=jax.ShapeDtypeStruct((N_TOK, D_MODEL), F32),
        scratch_shapes=[pltpu.VMEM((2, 2, TC_COMB, *ROW_TILE), F32),
                        pltpu.SemaphoreType.DMA((2,))],
        compiler_params=_cparams(1),
        name="combine",
    )(slot0, slot1, slot0, slot1, ys, x1, wts_rows, mods4, gf)


def _layer(x2, mods4, norm1_g, norm2_g, w_in, pool_w, pool_scale, w_branch_sb, w_branch_pool,
           w_out, w_rg, b_rg, w_re, b_re, w_gate, w_up, w_down, norm_f_g):
    q_scale = jnp.where(jnp.arange(MIX_PROJ_WIDTH) < SB_WIDTH, HEAD_DIM ** -0.5, 1.0).astype(F32)
    w_in_bf = (w_in * q_scale[None, :]).astype(BF16)
    proj = _inproj(x2, norm1_g.reshape(1, D_MODEL), mods4, w_in_bf)
    o_sb = _attention(proj)

    wr = jnp.concatenate([w_rg.T, jnp.zeros((4, D_MODEL), F32),
                          w_re.transpose(0, 2, 1).reshape(N_EXPERTS, D_MODEL)], axis=0)
    br = jnp.concatenate([b_rg, jnp.full((4,), NEG_BIG, F32), b_re.reshape(N_EXPERTS)]).reshape(ROUTE_ROWS, 1)
    x1, h2, logits_t = _merge(
        x2, o_sb, proj, mods4, norm2_g.reshape(1, D_MODEL), pool_w.astype(BF16),
        pool_scale.reshape(1, POOL_WIDTH), w_branch_sb.astype(BF16), w_branch_pool.astype(BF16),
        w_out.astype(BF16), wr.astype(BF16), br)

    info, slots, counts = _route(logits_t)
    slot0, slot1 = slots[0], slots[1]
    counts = counts.reshape(N_EXPERTS).astype(jnp.int32)
    pad_end = jnp.cumsum(((counts + TB_EXP - 1) // TB_EXP) * TB_EXP)
    block_first = jnp.arange(N_BLOCKS, dtype=jnp.int32) * TB_EXP
    block_e = jnp.minimum(jnp.sum((pad_end[None, :] <= block_first[:, None]).astype(jnp.int32), axis=1),
                          N_EXPERTS - 1).astype(jnp.int32)
    n_used = (pad_end[-1:] // TB_EXP).astype(jnp.int32)

    xs = _dispatch(pad_end.astype(jnp.int32), slot0, slot1, h2)
    ys = _experts(block_e, n_used, xs, w_gate, w_up, w_down)
    return _combine(slot0, slot1, ys, x1, info.T, mods4, norm_f_g.reshape(1, D_MODEL))


def kernel(x, c, norm1_g, norm2_g, w_ada, b_ada, w_in, pool_w, pool_scale, w_branch_sb, w_branch_pool, w_out, w_route_group, b_route_group, w_route_expert, b_route_expert, w_exp_gate, w_exp_up, w_exp_down, norm_f_g):
    assert x.shape == (BATCH, SEQ, D_MODEL) and w_ada.shape[0] == 1
    mods = _ada(c, w_ada[0], b_ada[0])
    mods4 = mods.reshape(BATCH, 6, 1, D_MODEL)
    out = _layer(x.reshape(N_TOK, D_MODEL), mods4, norm1_g[0], norm2_g[0], w_in[0], pool_w[0],
                 pool_scale[0], w_branch_sb[0], w_branch_pool[0], w_out[0], w_route_group[0],
                 b_route_group[0], w_route_expert[0], b_route_expert[0], w_exp_gate[0], w_exp_up[0],
                 w_exp_down[0], norm_f_g)
    return out.reshape(BATCH, SEQ, D_MODEL)
```

```python
import jax
import jax.numpy as jnp
from jax import lax
from jax.experimental import pallas as pl
from jax.experimental.pallas import tpu as pltpu

F32 = jnp.float32
BF16 = jnp.bfloat16

D_MODEL = 1024
BATCH = 16
SEQ = 2048
N_TOK = BATCH * SEQ
HEAD_DIM = 64
SB_WIDTH = 512
POOL_WIDTH = 512
POOL_WINDOWS = (2, 4, 8, 16)
POOL_GROUP_DIM = 128
MIX_PROJ_WIDTH = 4096
N_EXPERT_GROUPS = 4
EXPERTS_PER_GROUP = 8
N_EXPERTS = 32
D_EXPERT = 512
EPS = 1e-6

LANES = 128
SUBLANES = 8
ROW_TILE = (SUBLANES, D_MODEL // SUBLANES)
assert ROW_TILE[1] == LANES
HALO = 16
TM_PROJ = 512
TQ = 256
TK = 256
MXU_DIM = 256
ATTN_HEADS = 8
ATTN_WIDTH = ATTN_HEADS * HEAD_DIM
LOG2_E = 1.4426950408889634
UNDERFLOW_LOG2 = 160.0
TN_ROUTE = 512
TN_SLOT = 4096
TT_DISP = 2048
TB_EXP = 512
TC_COMB = 512
N_BLOCKS = (2 * N_TOK) // TB_EXP + N_EXPERTS
N_SLOTS = N_BLOCKS * TB_EXP
ROUTE_ROWS = 40
NEG_BIG = -1e30
V7X_VMEM_BYTES = 64 * 1024 * 1024
VMEM_LIMIT = V7X_VMEM_BYTES - 8 * 1024 * 1024


def _cparams(n_axes):
    return pltpu.CompilerParams(dimension_semantics=("arbitrary",) * n_axes,
                                vmem_limit_bytes=VMEM_LIMIT)


def _ada_kernel(c_ref, w_ref, b_ref, o_ref):
    c = c_ref[...]
    ca = (c * jax.nn.sigmoid(c)).astype(BF16)
    o_ref[...] = jnp.dot(ca, w_ref[...].astype(BF16), preferred_element_type=F32) + b_ref[...]


def _ada(c, w_ada, b_ada):
    n_out = w_ada.shape[1]
    tn = 1024
    return pl.pallas_call(
        _ada_kernel,
        grid=(n_out // tn,),
        in_specs=[pl.BlockSpec((BATCH, D_MODEL), lambda j: (0, 0)),
                  pl.BlockSpec((D_MODEL, tn), lambda j: (0, j)),
                  pl.BlockSpec((1, tn), lambda j: (0, j))],
        out_specs=pl.BlockSpec((BATCH, tn), lambda j: (0, j)),
        out_shape=jax.ShapeDtypeStruct((BATCH, n_out), F32),
        compiler_params=_cparams(1),
        name="ada",
    )(c, w_ada, b_ada.reshape(1, n_out))


def _rms_mod(x, g, shift, scale):
    ms = jnp.mean(x * x, axis=-1, keepdims=True)
    y = x * lax.rsqrt(ms + EPS) * g
    return y * (1.0 + scale) + shift


def _inproj_kernel(x_ref, g_ref, sh_ref, sc_ref, w_ref, o_ref):
    h = _rms_mod(x_ref[...], g_ref[...], sh_ref[...], sc_ref[...]).astype(BF16)
    o_ref[:, 0:SB_WIDTH] = (jnp.dot(h, w_ref[:, 0:SB_WIDTH], preferred_element_type=F32) * LOG2_E).astype(BF16)
    for lo in range(SB_WIDTH, MIX_PROJ_WIDTH, SB_WIDTH):
        o_ref[:, lo:lo + SB_WIDTH] = jnp.dot(
            h, w_ref[:, lo:lo + SB_WIDTH], preferred_element_type=F32).astype(BF16)


def _inproj(x2, g1, mods4, w_in_bf):
    tiles_per_seq = SEQ // TM_PROJ
    mod_spec = lambda k: pl.BlockSpec((None, None, 1, D_MODEL),
                                      lambda i: (i // tiles_per_seq, k, 0, 0))
    return pl.pallas_call(
        _inproj_kernel,
        grid=(N_TOK // TM_PROJ,),
        in_specs=[pl.BlockSpec((TM_PROJ, D_MODEL), lambda i: (i, 0)),
                  pl.BlockSpec((1, D_MODEL), lambda i: (0, 0)),
                  mod_spec(0), mod_spec(1),
                  pl.BlockSpec((D_MODEL, MIX_PROJ_WIDTH), lambda i: (0, 0))],
        out_specs=pl.BlockSpec((TM_PROJ, MIX_PROJ_WIDTH), lambda i: (i, 0)),
        out_shape=jax.ShapeDtypeStruct((N_TOK, MIX_PROJ_WIDTH), BF16),
        compiler_params=_cparams(1),
        name="inproj",
    )(x2, g1, mods4, mods4, w_in_bf)


def _attn_kernel(q_ref, k_ref, v_ref, o_ref, z_a, z_b, a_a, a_b, acc_ref, r_ref):
    qi = pl.program_id(2)
    head_q = lax.broadcasted_iota(jnp.int32, (TQ, MXU_DIM), 1) // HEAD_DIM
    head_k = lax.broadcasted_iota(jnp.int32, (TK, MXU_DIM), 1) // HEAD_DIM
    row = lax.broadcasted_iota(jnp.int32, (TQ, TK), 0)
    col = lax.broadcasted_iota(jnp.int32, (TQ, TK), 1)
    past = col < row
    upper = jnp.where(row > col, 1.0, 0.0).astype(BF16)
    heads_per_group = MXU_DIM // HEAD_DIM
    n_groups = ATTN_WIDTH // MXU_DIM

    def group_cols(g):
        return slice(g * MXU_DIM, (g + 1) * MXU_DIM)

    def scores(kb, z_out):
        start = pl.multiple_of(kb * TK, TK)
        for g in range(n_groups):
            qg = q_ref[:, group_cols(g)]
            k = k_ref[pl.ds(start, TK), group_cols(g)]
            for hh in range(heads_per_group):
                q_h = jnp.where(head_q == hh, qg, jnp.zeros_like(qg))
                z_out[g * heads_per_group + hh] = lax.dot_general(
                    q_h, k, (((1,), (1,)), ((), ())), preferred_element_type=F32)

    def weights(z_in, a_out, diag):
        for h in range(ATTN_HEADS):
            z2 = z_in[h]
            sp2 = jnp.maximum(z2, 0.0) + jnp.log(1.0 + jnp.exp2(-jnp.abs(z2))) * LOG2_E
            stay = jnp.where(past, sp2, 0.0) if diag else sp2
            cum = jnp.dot(stay.astype(BF16), upper, preferred_element_type=F32)
            a = jnp.exp2(z2 - (sp2 + cum + r_ref[h]))
            if diag:
                a = jnp.where(past, a, 0.0)
            a_out[h] = a.astype(BF16)
            r_ref[h] = r_ref[h] + jnp.sum(stay, axis=-1, keepdims=True)

    def accumulate(kb, a_in):
        start = pl.multiple_of(kb * TK, TK)
        for g in range(n_groups):
            v = v_ref[pl.ds(start, TK), group_cols(g)]
            acc = acc_ref[g]
            for hh in range(heads_per_group):
                v_h = jnp.where(head_k == hh, v, jnp.zeros_like(v))
                acc = acc + jnp.dot(a_in[g * heads_per_group + hh], v_h, preferred_element_type=F32)
            acc_ref[g] = acc

    acc_ref[...] = jnp.zeros_like(acc_ref)
    r_ref[...] = jnp.zeros_like(r_ref)
    scores(qi, z_a)
    scores(jnp.maximum(qi - 1, 0), z_b)
    weights(z_a, a_a, True)

    def step(carry):
        i, _ = carry
        kb = qi - i
        nxt = jnp.maximum(kb - 1, 0)

        @pl.when(i % 2 == 1)
        def _():
            accumulate(kb + 1, a_a)
            scores(nxt, z_a)
            weights(z_b, a_b, False)

        @pl.when(i % 2 == 0)
        def _():
            accumulate(kb + 1, a_b)
            scores(nxt, z_b)
            weights(z_a, a_a, False)
        return i + 1, jnp.min(r_ref[...]) >= UNDERFLOW_LOG2

    def first_older_block():
        accumulate(qi, a_a)
        weights(z_b, a_b, False)
        return (jnp.min(r_ref[...]) >= UNDERFLOW_LOG2).astype(jnp.int32)

    done = lax.cond(qi >= 1, first_older_block, lambda: jnp.int32(1))
    go_on = jnp.logical_and(qi >= 2, done == 0)

    @pl.when(go_on)
    def _():
        scores(qi - 2, z_a)

    n_done, _ = lax.while_loop(lambda c: jnp.logical_and(c[0] <= qi, jnp.logical_not(c[1])),
                               step, (jnp.where(qi >= 1, 2, 1).astype(jnp.int32), jnp.logical_not(go_on)))
    last = n_done - 1

    @pl.when(last % 2 == 1)
    def _():
        accumulate(qi - last, a_b)

    @pl.when(last % 2 == 0)
    def _():
        accumulate(qi - last, a_a)

    for g in range(n_groups):
        o_ref[:, group_cols(g)] = acc_ref[g].astype(BF16)


def _attention(proj):
    n_q = SEQ // TQ
    n_hg = SB_WIDTH // ATTN_WIDTH
    return pl.pallas_call(
        _attn_kernel,
        grid=(BATCH, n_hg, n_q),
        in_specs=[pl.BlockSpec((TQ, ATTN_WIDTH), lambda b, h, i: (b * n_q + i, h)),
                  pl.BlockSpec((SEQ, ATTN_WIDTH), lambda b, h, i: (b, n_hg + h)),
                  pl.BlockSpec((SEQ, ATTN_WIDTH), lambda b, h, i: (b, 2 * n_hg + h))],
        out_specs=pl.BlockSpec((TQ, ATTN_WIDTH), lambda b, h, i: (b * n_q + i, h)),
        out_shape=jax.ShapeDtypeStruct((N_TOK, SB_WIDTH), BF16),
        scratch_shapes=[pltpu.VMEM((ATTN_HEADS, TQ, TK), F32), pltpu.VMEM((ATTN_HEADS, TQ, TK), F32),
                        pltpu.VMEM((ATTN_HEADS, TQ, TK), BF16), pltpu.VMEM((ATTN_HEADS, TQ, TK), BF16),
                        pltpu.VMEM((ATTN_WIDTH // MXU_DIM, TQ, MXU_DIM), F32),
                        pltpu.VMEM((ATTN_HEADS, TQ, 1), F32)],
        compiler_params=_cparams(3),
        name="attn",
    )(proj, proj, proj)


def _merge_kernel(x_ref, osb_ref, p_ref, halo_ref, gsb_ref, gpool_ref,
                  poolw_ref, pscale_ref, wsb_ref, wpool_ref, wout_ref,
                  gate1_ref, shift2_ref, scale2_ref, g2_ref, wr_ref, br_ref,
                  x1_ref, h2_ref, lt_ref, pext_ref):
    i = pl.program_id(0)
    tiles_per_seq = SEQ // TM_PROJ
    t0 = (i % tiles_per_seq) * TM_PROJ
    halo = halo_ref[...].astype(F32)
    pext_ref[0:HALO, :] = jnp.where(t0 == 0, jnp.zeros_like(halo), halo)
    pext_ref[HALO:, :] = p_ref[...].astype(F32)
    pos = t0 + lax.broadcasted_iota(jnp.int32, (TM_PROJ, 1), 0)
    o_pool = []
    for g, w in enumerate(POOL_WINDOWS):
        cols = slice(g * POOL_GROUP_DIM, (g + 1) * POOL_GROUP_DIM)
        cur = pext_ref[HALO:, cols]
        win = cur
        for d in range(1, w):
            win = win + pext_ref[HALO - d:HALO - d + TM_PROJ, cols]
        count = jnp.minimum(pos + 1, w).astype(F32)
        pooled = win / count - cur
        og = jnp.dot(pooled.astype(BF16), poolw_ref[g], preferred_element_type=F32)
        o_pool.append(og * pscale_ref[:, cols])
    o_pool = jnp.concatenate(o_pool, axis=-1).astype(BF16)
    br_sb = jnp.dot(osb_ref[...], wsb_ref[...], preferred_element_type=F32)
    br_pool = jnp.dot(o_pool, wpool_ref[...], preferred_element_type=F32)
    merged = (jax.nn.sigmoid(gsb_ref[...].astype(F32)) * br_sb
              + jax.nn.sigmoid(gpool_ref[...].astype(F32)) * br_pool)
    mix = jnp.dot(merged.astype(BF16), wout_ref[...], preferred_element_type=F32)
    x1 = x_ref[...] + gate1_ref[...] * mix
    x1_ref[...] = x1
    h2 = _rms_mod(x1, g2_ref[...], shift2_ref[...], scale2_ref[...])
    h2_ref[...] = h2.reshape(TM_PROJ, *ROW_TILE)
    lt_ref[...] = lax.dot_general(wr_ref[...], h2.astype(BF16), (((1,), (1,)), ((), ())),
                                  preferred_element_type=F32) + br_ref[...]


def _merge(x2, o_sb, proj, mods4, g2, poolw_bf, pool_scale, wsb_bf, wpool_bf, wout_bf, wr_bf, br):
    tiles_per_seq = SEQ // TM_PROJ
    halo_blocks = TM_PROJ // HALO
    mod_spec = lambda k: pl.BlockSpec((None, None, 1, D_MODEL),
                                      lambda i: (i // tiles_per_seq, k, 0, 0))
    full = lambda shape: pl.BlockSpec(shape, lambda i: (0,) * len(shape))
    return pl.pallas_call(
        _merge_kernel,
        grid=(N_TOK // TM_PROJ,),
        in_specs=[pl.BlockSpec((TM_PROJ, D_MODEL), lambda i: (i, 0)),
                  pl.BlockSpec((TM_PROJ, SB_WIDTH), lambda i: (i, 0)),
                  pl.BlockSpec((TM_PROJ, POOL_WIDTH), lambda i: (i, 3)),
                  pl.BlockSpec((HALO, POOL_WIDTH),
                               lambda i: (jnp.maximum(i * halo_blocks - 1, 0), 3)),
                  pl.BlockSpec((TM_PROJ, D_MODEL), lambda i: (i, 2)),
                  pl.BlockSpec((TM_PROJ, D_MODEL), lambda i: (i, 3)),
                  full((len(POOL_WINDOWS), POOL_GROUP_DIM, POOL_GROUP_DIM)),
                  full((1, POOL_WIDTH)),
                  full((SB_WIDTH, D_MODEL)),
                  full((POOL_WIDTH, D_MODEL)),
                  full((D_MODEL, D_MODEL)),
                  mod_spec(2), mod_spec(3), mod_spec(4),
                  full((1, D_MODEL)),
                  full((ROUTE_ROWS, D_MODEL)),
                  full((ROUTE_ROWS, 1))],
        out_specs=[pl.BlockSpec((TM_PROJ, D_MODEL), lambda i: (i, 0)),
                   pl.BlockSpec((TM_PROJ, *ROW_TILE), lambda i: (i, 0, 0)),
                   pl.BlockSpec((ROUTE_ROWS, TM_PROJ), lambda i: (0, i))],
        out_shape=[jax.ShapeDtypeStruct((N_TOK, D_MODEL), F32),
                   jax.ShapeDtypeStruct((N_TOK, *ROW_TILE), F32),
                   jax.ShapeDtypeStruct((ROUTE_ROWS, N_TOK), F32)],
        scratch_shapes=[pltpu.VMEM((TM_PROJ + HALO, POOL_WIDTH), F32)],
        compiler_params=_cparams(1),
        name="merge",
    )(x2, o_sb, proj, proj, proj, proj, poolw_bf, pool_scale, wsb_bf, wpool_bf, wout_bf,
      mods4, mods4, mods4, g2, wr_bf, br)


def _first_index_of(mask, idx, big):
    return jnp.min(jnp.where(mask, idx, big), axis=0, keepdims=True)


def _route_kernel(lt_ref, info_ref, slot_ref, cnt_ref, base_ref, info_scr):
    step = pl.program_id(0)
    n_rank_steps = N_TOK // TN_ROUTE
    tn = TN_ROUTE

    @pl.when(step == 0)
    def _():
        base_ref[...] = jnp.zeros_like(base_ref)

    @pl.when(step < n_rank_steps)
    def _():
        tile = pl.ds(pl.multiple_of(step * tn, tn), tn)
        row8 = lax.broadcasted_iota(jnp.int32, (8, tn), 0).astype(F32)
        erow = lax.broadcasted_iota(jnp.int32, (N_EXPERTS, tn), 0).astype(F32)
        gl = lt_ref[0:8, :]
        gmax = jnp.max(gl, axis=0, keepdims=True)
        gsel = _first_index_of(gl == gmax, row8, 8.0)
        p_group = 1.0 / jnp.sum(jnp.exp(gl - gmax), axis=0, keepdims=True)
        ig = lt_ref[8:16, :]
        for g in range(1, N_EXPERT_GROUPS):
            ig = jnp.where(gsel == float(g), lt_ref[8 + 8 * g:16 + 8 * g, :], ig)
        m1 = jnp.max(ig, axis=0, keepdims=True)
        i1 = _first_index_of(ig == m1, row8, 8.0)
        rest = jnp.where(row8 == i1, -jnp.inf, ig)
        m2 = jnp.max(rest, axis=0, keepdims=True)
        i2 = _first_index_of(rest == m2, row8, 8.0)
        e2 = jnp.exp(m2 - m1)
        w_first = p_group / (1.0 + e2)
        w_second = p_group * e2 / (1.0 + e2)
        e_first = gsel * EXPERTS_PER_GROUP + i1
        e_second = gsel * EXPERTS_PER_GROUP + i2
        hit_first = erow == e_first
        hit_second = erow == e_second
        onehot = jnp.where(hit_first | hit_second, 1.0, 0.0)
        r_i = lax.broadcasted_iota(jnp.int32, (tn, tn), 0)
        c_i = lax.broadcasted_iota(jnp.int32, (tn, tn), 1)
        tri = jnp.where(r_i < c_i, 1.0, 0.0).astype(BF16)
        rank = jnp.dot(onehot.astype(BF16), tri, preferred_element_type=F32) + base_ref[...]
        rank_first = jnp.sum(jnp.where(hit_first, rank, 0.0), axis=0, keepdims=True)
        rank_second = jnp.sum(jnp.where(hit_second, rank, 0.0), axis=0, keepdims=True)
        base_ref[...] = base_ref[...] + jnp.sum(onehot, axis=1, keepdims=True)
        info = jnp.zeros((8, tn), F32)
        for r, val in enumerate((e_first, e_second, rank_first, rank_second, w_first, w_second)):
            info = jnp.where(row8 == float(r), val, info)
        info_scr[:, tile] = info

    @pl.when(step >= n_rank_steps)
    def _():
        ts = TN_SLOT
        tile = pl.ds(pl.multiple_of((step - n_rank_steps) * ts, ts), ts)
        row8 = lax.broadcasted_iota(jnp.int32, (8, ts), 0).astype(F32)
        erow = lax.broadcasted_iota(jnp.int32, (N_EXPERTS, ts), 0).astype(F32)
        counts = base_ref[...]
        padded = jnp.floor((counts + (TB_EXP - 1.0)) * (1.0 / TB_EXP)) * TB_EXP
        r_e = lax.broadcasted_iota(jnp.int32, (N_EXPERTS, N_EXPERTS), 0)
        c_e = lax.broadcasted_iota(jnp.int32, (N_EXPERTS, N_EXPERTS), 1)
        padded_row = jnp.sum(jnp.where(r_e == c_e, padded, 0.0), axis=0, keepdims=True)
        pad_start = jnp.sum(jnp.where(c_e < r_e, padded_row, 0.0), axis=1, keepdims=True)
        info = info_scr[:, tile]
        slot_first = jnp.sum(jnp.where(erow == info[0:1, :], pad_start, 0.0), axis=0, keepdims=True) + info[2:3, :]
        slot_second = jnp.sum(jnp.where(erow == info[1:2, :], pad_start, 0.0), axis=0, keepdims=True) + info[3:4, :]
        slots = jnp.where(row8 == 0.0, slot_first, jnp.where(row8 == 1.0, slot_second, 0.0))
        slot_ref[...] = slots.astype(jnp.int32)
        info_ref[...] = info
        cnt_ref[...] = counts


def _route(logits_t):
    n_rank_steps = N_TOK // TN_ROUTE
    n_slot_steps = N_TOK // TN_SLOT
    slot_blk = lambda s: (0, jnp.maximum(s - n_rank_steps, 0))
    return pl.pallas_call(
        _route_kernel,
        grid=(n_rank_steps + n_slot_steps,),
        in_specs=[pl.BlockSpec((ROUTE_ROWS, TN_ROUTE), lambda s: (0, jnp.minimum(s, n_rank_steps - 1)))],
        out_specs=[pl.BlockSpec((8, TN_SLOT), slot_blk),
                   pl.BlockSpec((8, TN_SLOT), slot_blk),
                   pl.BlockSpec((N_EXPERTS, 1), lambda s: (0, 0))],
        out_shape=[jax.ShapeDtypeStruct((8, N_TOK), F32),
                   jax.ShapeDtypeStruct((8, N_TOK), jnp.int32),
                   jax.ShapeDtypeStruct((N_EXPERTS, 1), F32)],
        scratch_shapes=[pltpu.VMEM((N_EXPERTS, 1), F32),
                        pltpu.VMEM((8, N_TOK), F32)],
        compiler_params=_cparams(1),
        name="route",
    )(logits_t)


def _dispatch_kernel(pend_ref, s0_ref, s1_ref, h2_ref, xs_hbm, zero_ref, sem, zero_sem):
    @pl.when(pl.program_id(0) == 0)
    def _():
        zero_ref[...] = jnp.zeros_like(zero_ref)
        clears = []
        for e in range(N_EXPERTS):
            end = pend_ref[e]
            has_block = end > (pend_ref[e - 1] if e > 0 else 0)
            start = pl.multiple_of(jnp.maximum(end - TB_EXP, 0), TB_EXP)
            clears.append((has_block, pltpu.make_async_copy(zero_ref, xs_hbm.at[pl.ds(start, TB_EXP)], zero_sem)))
        for t in range(N_BLOCKS - (2 * N_TOK) // TB_EXP):
            start = pl.multiple_of(jnp.minimum(pend_ref[N_EXPERTS - 1] + t * TB_EXP, N_SLOTS - TB_EXP), TB_EXP)
            unused = pend_ref[N_EXPERTS - 1] + t * TB_EXP < N_SLOTS
            clears.append((unused, pltpu.make_async_copy(zero_ref, xs_hbm.at[pl.ds(start, TB_EXP)], zero_sem)))
        for has_block, copy in clears:
            pl.when(has_block)(copy.start)
        for has_block, copy in clears:
            pl.when(has_block)(copy.wait)

    def row_copy(j, slot):
        return pltpu.make_async_copy(h2_ref.at[j], xs_hbm.at[slot], sem)

    def issue(j, c):
        row_copy(j, s0_ref[j]).start(priority=0)
        row_copy(j, s1_ref[j]).start(priority=1)
        return c

    lax.fori_loop(0, TT_DISP, issue, 0, unroll=8)

    def drain(j, c):
        row_copy(j, 0).wait()
        row_copy(j, 0).wait()
        return c

    lax.fori_loop(0, TT_DISP, drain, 0, unroll=8)


def _dispatch(pad_end, slot0, slot1, h2):
    smem = lambda: pl.BlockSpec((TT_DISP,), lambda i, pe: (i,), memory_space=pltpu.SMEM)
    return pl.pallas_call(
        _dispatch_kernel,
        grid_spec=pltpu.PrefetchScalarGridSpec(
            num_scalar_prefetch=1,
            grid=(N_TOK // TT_DISP,),
            in_specs=[smem(), smem(),
                      pl.BlockSpec((TT_DISP, *ROW_TILE), lambda i, pe: (i, 0, 0))],
            out_specs=pl.BlockSpec(memory_space=pl.ANY),
            scratch_shapes=[pltpu.VMEM((TB_EXP, *ROW_TILE), F32),
                            pltpu.SemaphoreType.DMA(()),
                            pltpu.SemaphoreType.DMA(())]),
        out_shape=jax.ShapeDtypeStruct((N_SLOTS, *ROW_TILE), F32),
        compiler_params=_cparams(1),
        name="dispatch",
    )(pad_end, slot0, slot1, h2)


def _experts_kernel(be_ref, nused_ref, xs_ref, wg_ref, wu_ref, wd_ref, ys_ref, wg_bf, wu_bf, wd_bf):
    i = pl.program_id(0)
    used = i < nused_ref[0]
    new_expert = jnp.logical_or(i == 0, be_ref[i] != be_ref[jnp.maximum(i - 1, 0)])

    @pl.when(jnp.logical_and(used, new_expert))
    def _():
        wg_bf[...] = wg_ref[...].astype(BF16)
        wu_bf[...] = wu_ref[...].astype(BF16)
        wd_bf[...] = wd_ref[...].astype(BF16)

    @pl.when(used)
    def _():
        x = xs_ref[...].reshape(TB_EXP, D_MODEL).astype(BF16)
        g = jnp.dot(x, wg_bf[...], preferred_element_type=F32)
        u = jnp.dot(x, wu_bf[...], preferred_element_type=F32)
        mid = (g * jax.nn.sigmoid(g) * u).astype(BF16)
        y = jnp.dot(mid, wd_bf[...], preferred_element_type=F32)
        ys_ref[...] = y.reshape(TB_EXP, *ROW_TILE)

    @pl.when(jnp.logical_not(used))
    def _():
        ys_ref[...] = jnp.zeros_like(ys_ref)


def _experts(block_e, n_used, xs, w_gate, w_up, w_down):
    blk = lambda i, be, nu: (jnp.minimum(i, nu[0] - 1), 0, 0)
    wsel = lambda i, be, nu: (be[i], 0, 0)
    return pl.pallas_call(
        _experts_kernel,
        grid_spec=pltpu.PrefetchScalarGridSpec(
            num_scalar_prefetch=2,
            grid=(N_BLOCKS,),
            in_specs=[pl.BlockSpec((TB_EXP, *ROW_TILE), blk),
                      pl.BlockSpec((None, D_MODEL, D_EXPERT), wsel),
                      pl.BlockSpec((None, D_MODEL, D_EXPERT), wsel),
                      pl.BlockSpec((None, D_EXPERT, D_MODEL), wsel)],
            out_specs=pl.BlockSpec((TB_EXP, *ROW_TILE), lambda i, be, nu: (i, 0, 0)),
            scratch_shapes=[pltpu.VMEM((D_MODEL, D_EXPERT), BF16),
                            pltpu.VMEM((D_MODEL, D_EXPERT), BF16),
                            pltpu.VMEM((D_EXPERT, D_MODEL), BF16)]),
        out_shape=jax.ShapeDtypeStruct((N_SLOTS, *ROW_TILE), F32),
        compiler_params=_cparams(1),
        name="experts",
    )(block_e, n_used, xs, w_gate, w_up, w_down)


def _combine_kernel(s0_ref, s1_ref, s0_next_ref, s1_next_ref, ys_hbm, x1_ref, wts_ref, gate2_ref, gf_ref,
                    o_ref, y_ref, sem):
    i = pl.program_id(0)
    cur = i % 2

    def row_copy(slot, buf, pick, j):
        return pltpu.make_async_copy(ys_hbm.at[slot], y_ref.at[buf, pick, j], sem.at[buf])

    def issue(first_ref, second_ref, buf):
        def body(j, c):
            row_copy(first_ref[j], buf, 0, j).start(priority=0)
            row_copy(second_ref[j], buf, 1, j).start(priority=1)
            return c
        lax.fori_loop(0, TC_COMB, body, 0, unroll=8)

    @pl.when(i == 0)
    def _():
        issue(s0_ref, s1_ref, 0)

    @pl.when(i + 1 < pl.num_programs(0))
    def _():
        issue(s0_next_ref, s1_next_ref, 1 - cur)

    def drain(j, c):
        row_copy(0, cur, 0, j).wait()
        row_copy(0, cur, 1, j).wait()
        return c

    lax.fori_loop(0, TC_COMB, drain, 0, unroll=8)
    wts = wts_ref[...]
    y_first = y_ref[cur, 0].reshape(TC_COMB, D_MODEL)
    y_second = y_ref[cur, 1].reshape(TC_COMB, D_MODEL)
    moe = wts[:, 4:5] * y_first + wts[:, 5:6] * y_second
    x2 = x1_ref[...] + gate2_ref[...] * moe
    ms = jnp.mean(x2 * x2, axis=-1, keepdims=True)
    o_ref[...] = x2 * lax.rsqrt(ms + EPS) * gf_ref[...]


def _combine(slot0, slot1, ys, x1, wts_rows, mods4, gf):
    n_steps = N_TOK // TC_COMB
    tiles_per_seq = SEQ // TC_COMB
    smem = lambda: pl.BlockSpec((TC_COMB,), lambda i: (i,), memory_space=pltpu.SMEM)
    smem_next = lambda: pl.BlockSpec((TC_COMB,), lambda i: (jnp.minimum(i + 1, n_steps - 1),),
                                     memory_space=pltpu.SMEM)
    return pl.pallas_call(
        _combine_kernel,
        grid=(n_steps,),
        in_specs=[smem(), smem(), smem_next(), smem_next(),
                  pl.BlockSpec(memory_space=pl.ANY),
                  pl.BlockSpec((TC_COMB, D_MODEL), lambda i: (i, 0)),
                  pl.BlockSpec((TC_COMB, 8), lambda i: (i, 0)),
                  pl.BlockSpec((None, None, 1, D_MODEL), lambda i: (i // tiles_per_seq, 5, 0, 0)),
                  pl.BlockSpec((1, D_MODEL), lambda i: (0, 0))],
        out_specs=pl.BlockSpec((TC_COMB, D_MODEL), lambda i: (i, 0)),
        out_shape=jax.ShapeDtypeStruct((N_TOK, D_MODEL), F32),
        scratch_shapes=[pltpu.VMEM((2, 2, TC_COMB, *ROW_TILE), F32),
                        pltpu.SemaphoreType.DMA((2,))],
        compiler_params=_cparams(1),
        name="combine",
    )(slot0, slot1, slot0, slot1, ys, x1, wts_rows, mods4, gf)


def _layer(x2, mods4, norm1_g, norm2_g, w_in, pool_w, pool_scale, w_branch_sb, w_branch_pool,
           w_out, w_rg, b_rg, w_re, b_re, w_gate, w_up, w_down, norm_f_g):
    q_scale = jnp.where(jnp.arange(MIX_PROJ_WIDTH) < SB_WIDTH, HEAD_DIM ** -0.5, 1.0).astype(F32)
    w_in_bf = (w_in * q_scale[None, :]).astype(BF16)
    proj = _inproj(x2, norm1_g.reshape(1, D_MODEL), mods4, w_in_bf)
    o_sb = _attention(proj)

    wr = jnp.concatenate([w_rg.T, jnp.zeros((4, D_MODEL), F32),
                          w_re.transpose(0, 2, 1).reshape(N_EXPERTS, D_MODEL)], axis=0)
    br = jnp.concatenate([b_rg, jnp.full((4,), NEG_BIG, F32), b_re.reshape(N_EXPERTS)]).reshape(ROUTE_ROWS, 1)
    x1, h2, logits_t = _merge(
        x2, o_sb, proj, mods4, norm2_g.reshape(1, D_MODEL), pool_w.astype(BF16),
        pool_scale.reshape(1, POOL_WIDTH), w_branch_sb.astype(BF16), w_branch_pool.astype(BF16),
        w_out.astype(BF16), wr.astype(BF16), br)

    info, slots, counts = _route(logits_t)
    slot0, slot1 = slots[0], slots[1]
    counts = counts.reshape(N_EXPERTS).astype(jnp.int32)
    pad_end = jnp.cumsum(((counts + TB_EXP - 1) // TB_EXP) * TB_EXP)
    block_first = jnp.arange(N_BLOCKS, dtype=jnp.int32) * TB_EXP
    block_e = jnp.minimum(jnp.sum((pad_end[None, :] <= block_first[:, None]).astype(jnp.int32), axis=1),
                          N_EXPERTS - 1).astype(jnp.int32)
    n_used = (pad_end[-1:] // TB_EXP).astype(jnp.int32)

    xs = _dispatch(pad_end.astype(jnp.int32), slot0, slot1, h2)
    ys = _experts(block_e, n_used, xs, w_gate, w_up, w_down)
    return _combine(slot0, slot1, ys, x1, info.T, mods4, norm_f_g.reshape(1, D_MODEL))


def kernel(x, c, norm1_g, norm2_g, w_ada, b_ada, w_in, pool_w, pool_scale, w_branch_sb, w_branch_pool, w_out, w_route_group, b_route_group, w_route_expert, b_route_expert, w_exp_gate, w_exp_up, w_exp_down, norm_f_g):
    assert x.shape == (BATCH, SEQ, D_MODEL) and w_ada.shape[0] == 1
    mods = _ada(c, w_ada[0], b_ada[0])
    mods4 = mods.reshape(BATCH, 6, 1, D_MODEL)
    out = _layer(x.reshape(N_TOK, D_MODEL), mods4, norm1_g[0], norm2_g[0], w_in[0], pool_w[0],
                 pool_scale[0], w_branch_sb[0], w_branch_pool[0], w_out[0], w_route_group[0],
                 b_route_group[0], w_route_expert[0], b_route_expert[0], w_exp_gate[0], w_exp_up[0],
                 w_exp_down[0], norm_f_g)
    return out.reshape(BATCH, SEQ, D_MODEL)
```
